```python
import math
import jax, jax.numpy as jnp
from jax import lax
import numpy as np

D_MODEL = 1024
BATCH = 8
SEQ = 4096
DEPTH = 2
DEC_BATCH = 8
DEC_SEQ = 64
PAST_LEN = 1024

CHUNK = 64
D_MIX = D_MODEL
D_A = D_MIX // 2
DK_A = 128
DV_A = 128
H_A = D_A // DV_A
A_CONV = 4
D_B = D_MIX // 4
B_BLOCKS = 4
B_BW = D_B // B_BLOCKS
B_CONV = 4
RG_C = 8.0
D_C = D_MIX - D_A - D_B
H_C = 4
DK_C = D_C // H_C
DV_C = D_C // H_C
N_MEM = 256
MEM_HEADS = 4
MEM_HEAD_DIM = D_MODEL // MEM_HEADS
D_FF = 2816
FFN_CONV = 3
RMS_EPS = 1e-6
IN_SIZES = (3 * D_A, D_A, H_A, H_A, D_B, D_B, H_C * DK_C, H_C * DK_C, D_C, D_C)
N_IN = sum(IN_SIZES)

kernel_name = "hybrid_gdn_rglru_hgrn2_stream_step"


def rmsnorm(x, w):
    xf = x.astype(jnp.float32)
    y = xf * lax.rsqrt(jnp.mean(xf * xf, axis=-1, keepdims=True) + RMS_EPS)
    return (y * w.astype(jnp.float32)).astype(x.dtype)


def l2norm(x):
    xf = x.astype(jnp.float32)
    return xf * lax.rsqrt(jnp.sum(xf * xf, axis=-1, keepdims=True) + 1e-6)


def causal_dwconv(x, prev, w):
    width, length = w.shape[0], x.shape[1]
    xp = jnp.concatenate([prev.astype(x.dtype), x], axis=1)
    y = xp[:, 0:length] * w[0]
    for j in range(1, width):
        y = y + xp[:, j:j + length] * w[j]
    return y, xp[:, length:]


def to_chunks(t, c):
    b, l = t.shape[:2]
    t = t.reshape((b, l // c, c) + t.shape[2:])
    return jnp.moveaxis(jnp.moveaxis(t, 1, 0), 2, 3)


def from_chunks(t):
    t = jnp.moveaxis(jnp.moveaxis(t, 3, 2), 0, 1)
    return t.reshape((t.shape[0], t.shape[1] * t.shape[2]) + t.shape[3:])


def gated_delta_rule(q, k, v, g, beta, s0):
    c = math.gcd(q.shape[1], CHUNK)
    q, k, v = to_chunks(q, c), to_chunks(k, c), to_chunks(v, c)
    g, beta = to_chunks(g, c), to_chunks(beta, c)
    incl = jnp.tril(jnp.ones((c, c), dtype=bool))
    strict = jnp.tril(jnp.ones((c, c), dtype=bool), -1)
    gcum = jnp.cumsum(g, axis=-1)
    gamma = jnp.exp(jnp.where(incl, gcum[..., :, None] - gcum[..., None, :], -jnp.inf))
    kk = jnp.einsum('nbhid,nbhjd->nbhij', k, k)
    m = jnp.eye(c, dtype=q.dtype) + jnp.where(strict, beta[..., :, None] * kk * gamma, 0.0)
    rhs = jnp.concatenate([v * beta[..., None], k * (beta * jnp.exp(gcum))[..., None]], axis=-1)
    sol = lax.linalg.triangular_solve(m, rhs, left_side=True, lower=True, unit_diagonal=True)
    dv = v.shape[-1]
    u, w = sol[..., :dv], sol[..., dv:]
    qk = jnp.einsum('nbhid,nbhjd->nbhij', q, k) * gamma
    q_dec = q * jnp.exp(gcum)[..., None]
    k_dec = k * jnp.exp(gcum[..., -1:] - gcum)[..., None]
    g_last = jnp.exp(gcum[..., -1])

    def step(s, inp):
        u_c, w_c, qk_c, qd_c, kd_c, gl_c = inp
        v_new = u_c - jnp.einsum('bhcd,bhdv->bhcv', w_c, s)
        o = jnp.einsum('bhcd,bhdv->bhcv', qd_c, s) + jnp.einsum('bhij,bhjv->bhiv', qk_c, v_new)
        s = gl_c[..., None, None] * s + jnp.einsum('bhcd,bhcv->bhdv', kd_c, v_new)
        return s, o

    s_last, o = lax.scan(step, s0, (u, w, qk, q_dec, k_dec, g_last))
    return from_chunks(o), s_last


def hgrn2_recurrence(q, k, v, log_f, s0):
    c = math.gcd(q.shape[1], CHUNK)
    q, k, v, log_f = [to_chunks(t, c) for t in (q, k, v, log_f)]
    incl = jnp.tril(jnp.ones((c, c), dtype=bool))

    def step(s, inp):
        q_c, k_c, v_c, lf_c = inp
        bc = jnp.cumsum(lf_c, axis=2)
        decay = jnp.exp(jnp.where(incl[..., None], bc[:, :, :, None, :] - bc[:, :, None, :, :], -jnp.inf))
        att = jnp.einsum('bhid,bhjd,bhijd->bhij', q_c, k_c, decay)
        o = jnp.einsum('bhij,bhjv->bhiv', att, v_c) + jnp.einsum('bhid,bhdv->bhiv', q_c * jnp.exp(bc), s)
        b_last = bc[:, :, -1:, :]
        s = jnp.exp(b_last[:, :, 0, :])[..., None] * s + jnp.einsum('bhjd,bhjv->bhdv', k_c * jnp.exp(b_last - bc), v_c)
        return s, o

    s_last, o = lax.scan(step, s0, (q, k, v, log_f))
    return from_chunks(o), s_last


def rg_lru_scan(a, u, h0):
    u = u.at[:, 0].add(a[:, 0] * h0)

    def combine(left, right):
        return left[0] * right[0], right[0] * left[1] + right[1]

    _, h = lax.associative_scan(combine, (a, u), axis=1)
    return h, h[:, -1]


def mixing_block(h, a_conv_prev, a_s, b_conv_prev, b_h, c_s,
                 w_in, a_conv_w, a_A_log, a_dt_bias, a_norm,
                 b_conv_w, b_conv_b, b_w_r, b_b_r, b_w_i, b_b_i, b_lambda,
                 c_lb, c_norm, w_out):
    bsz, length, _ = h.shape
    f32 = jnp.float32
    dt = h.dtype
    proj = h @ w_in
    splits = np.cumsum(IN_SIZES)[:-1].tolist()
    a_qkv, a_z, a_b, a_a, b_x, b_gate, c_q, c_f, c_i, c_g = jnp.split(proj, splits, axis=-1)

    qkv, a_conv_new = causal_dwconv(a_qkv, a_conv_prev, a_conv_w)
    qkv = jax.nn.silu(qkv)
    q, k, v = [t.reshape(bsz, length, H_A, -1) for t in jnp.split(qkv, 3, axis=-1)]
    q = l2norm(q) * (DK_A ** -0.5)
    k = l2norm(k)
    beta = jax.nn.sigmoid(a_b.astype(f32))
    g = -jnp.exp(a_A_log.astype(f32)) * jax.nn.softplus(a_a.astype(f32) + a_dt_bias.astype(f32))
    o_a, a_s_new = gated_delta_rule(q, k, v.astype(f32), g, beta, a_s.astype(f32))
    o_a = rmsnorm(o_a, a_norm) * jax.nn.silu(a_z.astype(f32).reshape(bsz, length, H_A, DV_A))
    o_a = o_a.reshape(bsz, length, D_A).astype(dt)

    xb, b_conv_new = causal_dwconv(b_x, b_conv_prev, b_conv_w)
    xb = (xb + b_conv_b).astype(f32)
    xblk = xb.reshape(bsz, length, B_BLOCKS, B_BW)
    r = jax.nn.sigmoid(jnp.einsum('blni,nij->blnj', xblk, b_w_r.astype(f32)).reshape(bsz, length, D_B) + b_b_r)
    i_gate = jax.nn.sigmoid(jnp.einsum('blni,nij->blnj', xblk, b_w_i.astype(f32)).reshape(bsz, length, D_B) + b_b_i)
    log_a = -RG_C * r * jax.nn.softplus(-b_lambda.astype(f32))
    mult = jnp.sqrt(jnp.maximum(-jnp.expm1(2.0 * log_a), 0.0))
    hb, b_h_new = rg_lru_scan(jnp.exp(log_a), mult * i_gate * xb, b_h.astype(f32))
    y_b = (hb * jax.nn.gelu(b_gate.astype(f32))).astype(dt)

    q_c = jax.nn.silu(c_q.astype(f32)).reshape(bsz, length, H_C, DK_C)
    lb = c_lb.reshape(H_C, DK_C)
    forget = lb + (1.0 - lb) * jax.nn.sigmoid(c_f.astype(f32).reshape(bsz, length, H_C, DK_C))
    v_c = c_i.astype(f32).reshape(bsz, length, H_C, DV_C)
    o_c, c_s_new = hgrn2_recurrence(q_c, 1.0 - forget, v_c, jnp.log(forget), c_s.astype(f32))
    o_c = rmsnorm(o_c, c_norm) * jax.nn.silu(c_g.astype(f32).reshape(bsz, length, H_C, DV_C))
    o_c = o_c.reshape(bsz, length, D_C).astype(dt)

    out = jnp.concatenate([o_a, y_b, o_c], axis=-1) @ w_out
    return out, (a_conv_new, a_s_new.astype(dt), b_conv_new, b_h_new.astype(dt), c_s_new.astype(dt))


def memory_kv(mem, norm_mem, w_k, w_v):
    bsz = mem.shape[0]
    mn = rmsnorm(mem, norm_mem)
    k = (mn @ w_k).reshape(bsz, -1, MEM_HEADS, MEM_HEAD_DIM)
    v = (mn @ w_v).reshape(bsz, -1, MEM_HEADS, MEM_HEAD_DIM)
    return k, v


def memory_cross_attention(h, mem_k, mem_v, w_q, w_o):
    bsz, length, _ = h.shape
    q = (h @ w_q).reshape(bsz, length, MEM_HEADS, MEM_HEAD_DIM)
    s = jnp.einsum('blhd,bmhd->bhlm', q, mem_k).astype(jnp.float32) * (MEM_HEAD_DIM ** -0.5)
    p = jax.nn.softmax(s, axis=-1).astype(h.dtype)
    o = jnp.einsum('bhlm,bmhd->blhd', p, mem_v).reshape(bsz, length, D_MODEL)
    return o @ w_o


def conv_ffn(h, conv_prev, w_up, conv_w, conv_b, w_down):
    u, conv_new = causal_dwconv(h @ w_up, conv_prev, conv_w)
    gate, val = jnp.split(u + conv_b, 2, axis=-1)
    return (jax.nn.silu(gate) * val) @ w_down, conv_new


def setup_inputs(seed: int = 0) -> dict:
    key = jax.random.key(seed)
    ks = jax.random.split(key, 48)
    idx = iter(range(48))

    def nrm(shape, scale=1.0):
        return scale * jax.random.normal(ks[next(idx)], shape, jnp.float32)

    def gain(shape):
        return 1.0 + nrm(shape, 0.02)

    def unif(shape, lo, hi):
        return jax.random.uniform(ks[next(idx)], shape, jnp.float32, lo, hi)

    a_A_log = jnp.log(unif((DEPTH, H_A), 1.0, 16.0))
    dt = jnp.exp(unif((DEPTH, H_A), math.log(1e-3), math.log(1e-1)))
    a_dt_bias = dt + jnp.log(-jnp.expm1(-dt))
    lru_s = unif((DEPTH, D_B), 0.9, 0.999) ** (1.0 / RG_C)
    b_lambda = jnp.log(lru_s) - jnp.log1p(-lru_s)
    return {
        'x_prompt': nrm((BATCH, SEQ, D_MODEL)),
        'x_sample': nrm((DEC_BATCH, DEC_SEQ, D_MODEL)),
        'cache_mem_k': nrm((DEPTH, DEC_BATCH, N_MEM, MEM_HEADS, MEM_HEAD_DIM)),
        'cache_mem_v': nrm((DEPTH, DEC_BATCH, N_MEM, MEM_HEADS, MEM_HEAD_DIM)),
        'state_a_conv': nrm((DEPTH, DEC_BATCH, A_CONV - 1, 3 * D_A)),
        'state_a_S': nrm((DEPTH, DEC_BATCH, H_A, DK_A, DV_A), 0.1),
        'state_b_conv': nrm((DEPTH, DEC_BATCH, B_CONV - 1, D_B)),
        'state_b_h': nrm((DEPTH, DEC_BATCH, D_B), 0.5),
        'state_c_S': nrm((DEPTH, DEC_BATCH, H_C, DK_C, DV_C), 0.5),
        'state_ffn_conv': nrm((DEPTH, DEC_BATCH, FFN_CONV - 1, 2 * D_FF)),
        'mem_prompt': nrm((BATCH, N_MEM, D_MODEL)),
        'norm_mix': gain((DEPTH, D_MODEL)),
        'w_in': nrm((DEPTH, D_MODEL, N_IN), D_MODEL ** -0.5),
        'a_conv_w': nrm((DEPTH, A_CONV, 3 * D_A), A_CONV ** -0.5),
        'a_A_log': a_A_log,
        'a_dt_bias': a_dt_bias,
        'a_norm': gain((DEPTH, DV_A)),
        'b_conv_w': nrm((DEPTH, B_CONV, D_B), B_CONV ** -0.5),
        'b_conv_b': nrm((DEPTH, D_B), 0.02),
        'b_w_r': nrm((DEPTH, B_BLOCKS, B_BW, B_BW), B_BW ** -0.5),
        'b_b_r': nrm((DEPTH, D_B), 0.02),
        'b_w_i': nrm((DEPTH, B_BLOCKS, B_BW, B_BW), B_BW ** -0.5),
        'b_b_i': nrm((DEPTH, D_B), 0.02),
        'b_lambda': b_lambda,
        'c_lb_logits': nrm((DEPTH, D_C), 0.1),
        'c_norm': gain((DEPTH, DV_C)),
        'w_out': nrm((DEPTH, D_MIX, D_MODEL), D_MIX ** -0.5),
        'norm_attn': gain((DEPTH, D_MODEL)),
        'norm_mem': gain((DEPTH, D_MODEL)),
        'w_q': nrm((DEPTH, D_MODEL, D_MODEL), D_MODEL ** -0.5),
        'w_k': nrm((DEPTH, D_MODEL, D_MODEL), D_MODEL ** -0.5),
        'w_v': nrm((DEPTH, D_MODEL, D_MODEL), D_MODEL ** -0.5),
        'w_o': nrm((DEPTH, D_MODEL, D_MODEL), D_MODEL ** -0.5),
        'norm_ffn': gain((DEPTH, D_MODEL)),
        'w_up': nrm((DEPTH, D_MODEL, 2 * D_FF), D_MODEL ** -0.5),
        'ffn_conv_w': nrm((DEPTH, FFN_CONV, 2 * D_FF), FFN_CONV ** -0.5),
        'ffn_conv_b': nrm((DEPTH, 2 * D_FF), 0.02),
        'w_down': nrm((DEPTH, D_FF, D_MODEL), D_FF ** -0.5),
        'norm_final': gain((D_MODEL,)),
    }


def reference(x_prompt, x_sample, cache_mem_k, cache_mem_v, state_a_conv, state_a_S, state_b_conv,
              state_b_h, state_c_S, state_ffn_conv, mem_prompt, norm_mix, w_in, a_conv_w, a_A_log,
              a_dt_bias, a_norm, b_conv_w, b_conv_b, b_w_r, b_b_r, b_w_i, b_b_i, b_lambda,
              c_lb_logits, c_norm, w_out, norm_attn, norm_mem, w_q, w_k, w_v, w_o, norm_ffn,
              w_up, ffn_conv_w, ffn_conv_b, w_down, norm_final):
    sm = jax.nn.softmax(c_lb_logits.astype(jnp.float32), axis=0)
    lower_bounds = jnp.cumsum(sm, axis=0) - sm[0]

    def layer(l, x, mem_k, mem_v, a_conv, a_s, b_conv, b_h, c_s, f_conv):
        mix, mix_state = mixing_block(
            rmsnorm(x, norm_mix[l]), a_conv, a_s, b_conv, b_h, c_s,
            w_in[l], a_conv_w[l], a_A_log[l], a_dt_bias[l], a_norm[l],
            b_conv_w[l], b_conv_b[l], b_w_r[l], b_b_r[l], b_w_i[l], b_b_i[l], b_lambda[l],
            lower_bounds[l], c_norm[l], w_out[l])
        x = x + mix
        x = x + memory_cross_attention(rmsnorm(x, norm_attn[l]), mem_k, mem_v, w_q[l], w_o[l])
        f, f_conv_new = conv_ffn(rmsnorm(x, norm_ffn[l]), f_conv, w_up[l], ffn_conv_w[l], ffn_conv_b[l], w_down[l])
        return x + f, mix_state + (f_conv_new,)

    x = x_prompt
    bsz, dt = x.shape[0], x.dtype

    def zeros(*shape):
        return jnp.zeros((bsz,) + shape, dt)

    prompt_kv, prompt_states = [], []
    for l in range(DEPTH):
        mk, mv = memory_kv(mem_prompt, norm_mem[l], w_k[l], w_v[l])
        x, st = layer(l, x, mk, mv, zeros(A_CONV - 1, 3 * D_A), zeros(H_A, DK_A, DV_A),
                      zeros(B_CONV - 1, D_B), zeros(D_B), zeros(H_C, DK_C, DV_C),
                      zeros(FFN_CONV - 1, 2 * D_FF))
        prompt_kv.append((mk, mv))
        prompt_states.append(st)
    y_prompt = rmsnorm(x, norm_final)
    p_mem_k, p_mem_v = [jnp.stack(t, axis=0) for t in zip(*prompt_kv)]
    pa_conv, pa_s, pb_conv, pb_h, pc_s, pf_conv = [jnp.stack(t, axis=0) for t in zip(*prompt_states)]

    x = x_sample
    sample_states = []
    for l in range(DEPTH):
        x, st = layer(l, x, cache_mem_k[l], cache_mem_v[l], state_a_conv[l], state_a_S[l],
                      state_b_conv[l], state_b_h[l], state_c_S[l], state_ffn_conv[l])
        sample_states.append(st)
    y_sample = rmsnorm(x, norm_final)
    sa_conv, sa_s, sb_conv, sb_h, sc_s, sf_conv = [jnp.stack(t, axis=0) for t in zip(*sample_states)]

    return (y_prompt, y_sample, p_mem_k, p_mem_v, pa_conv, pa_s, pb_conv, pb_h, pc_s, pf_conv,
            sa_conv, sa_s, sb_conv, sb_h, sc_s, sf_conv)
```

```python
import functools
import math

import jax
import jax.numpy as jnp
from jax import lax
from jax.experimental import pallas as pl
from jax.experimental.pallas import tpu as pltpu

F32 = jnp.float32
BF16 = jnp.bfloat16
HI = lax.Precision.HIGHEST

D_MODEL = 1024
CHUNK = 64
D_A = 512
DK_A = 128
DV_A = 128
H_A = 4
A_CONV = 4
D_B = 256
B_BLOCKS = 4
B_CONV = 4
RG_C = 8.0
D_C = 256
H_C = 4
DK_C = 64
N_MEM = 256
MEM_HEADS = 4
MEM_HEAD_DIM = 256
D_FF = 2816
FFN_CONV = 3
RMS_EPS = 1e-6

LANES = 128
SUBLANES = 8
VMEM_LIMIT_BYTES = 56 * 1024 * 1024

OFF_QKV = 0
OFF_Z = 1536
OFF_BA = 2048
OFF_BX = 2176
OFF_BG = 2432
OFF_CQ = 2688
OFF_CF = 2944
OFF_CI = 3200
OFF_CG = 3456
N_PROJ = 3712
OFF_CK = 3712
OFF_G = 3968
PROJ_W = 4096
FFN_BLK = 256
N_FFN_BLK = D_FF // FFN_BLK


def _mm(a, b):
    return jnp.dot(a.astype(BF16), b.astype(BF16), preferred_element_type=F32)


def _mm_nt(a, b):
    return lax.dot_general(a.astype(BF16), b.astype(BF16), (((1,), (1,)), ((), ())),
                           preferred_element_type=F32)


def _mm_tn(a, b):
    return lax.dot_general(a.astype(BF16), b.astype(BF16), (((0,), (0,)), ((), ())),
                           preferred_element_type=F32)


def _mmh(a, b):
    return jnp.dot(a, b, precision=HI, preferred_element_type=F32)


def _mmh_nt(a, b):
    return lax.dot_general(a, b, (((1,), (1,)), ((), ())), precision=HI,
                           preferred_element_type=F32)


def _rms(x, g):
    ms = jnp.mean(x * x, axis=-1, keepdims=True)
    return x * lax.rsqrt(ms + RMS_EPS) * g


def _silu(x):
    return x * jax.nn.sigmoid(x)


def _iota(shape, axis):
    return lax.broadcasted_iota(jnp.int32, shape, axis)


def _shift_rows(x, carry8, s):
    rolled = pltpu.roll(x, s, 0)
    head = jnp.where(_iota(carry8.shape, 0) < s, pltpu.roll(carry8, s, 0), rolled[0:SUBLANES])
    if x.shape[0] == SUBLANES:
        return head
    return jnp.concatenate([head, rolled[SUBLANES:]], axis=0)


def _causal_conv(x, carry8, w):
    width = w.shape[0]
    y = x * w[width - 1:width, :]
    for s in range(1, width):
        y = y + _shift_rows(x, carry8, s) * w[width - 1 - s:width - s, :]
    return y


def _lin_scan(a, u):
    n = a.shape[0]
    row = _iota(a.shape, 0)
    s = 1
    while s < n:
        keep = row >= s
        a_s = jnp.where(keep, pltpu.roll(a, s, 0), 1.0)
        u_s = jnp.where(keep, pltpu.roll(u, s, 0), 0.0)
        u = a * u_s + u
        a = a * a_s
        s *= 2
    return a, u


def _stack_heads(x, width):
    n = x.shape[1] // width
    return jnp.concatenate([x[:, h * width:(h + 1) * width] for h in range(n)], axis=0)


def _gdn_chunk(q, k, v, beta_blk, g_blk, s_ref):
    c = CHUNK
    n = H_A * c
    tri = (_iota((c, c), 1) <= _iota((c, c), 0)).astype(F32)
    gcum = _mmh(tri, g_blk)
    qs = _stack_heads(q, DK_A)
    ks = _stack_heads(k, DK_A)
    vs = _stack_heads(v, DV_A)
    gc = jnp.concatenate([gcum[:, 4 + h:5 + h] for h in range(H_A)], axis=0)
    bcol = jnp.concatenate([beta_blk[:, h:h + 1] for h in range(H_A)], axis=0)
    glast = jnp.concatenate(
        [jnp.broadcast_to(gcum[c - 1:c, 4 + h:5 + h], (c, 1)) for h in range(H_A)], axis=0)
    gr = jnp.broadcast_to(gc, (n, LANES)).T[0:1, :]
    row = _iota((n, n), 0)
    col = _iota((n, n), 1)
    same = (row // c) == (col // c)
    incl = same & (col <= row)
    strict = same & (col < row)
    gamma = jnp.exp(jnp.where(incl, gc - gr, -jnp.inf))
    kk = _mm_nt(ks, ks)
    a = jnp.where(strict, bcol * kk * gamma, 0.0)
    x = jnp.where(row == col, 1.0, 0.0) - a
    p = _mmh(a, a)
    for it in range(5):
        x = x + _mmh(x, p)
        if it < 4:
            p = _mmh(p, p)
    eg = jnp.exp(gc)
    rhs = jnp.concatenate([vs * bcol, ks * (bcol * eg)], axis=1)
    sol = _mmh(x, rhs)
    u = sol[:, :DV_A]
    w = sol[:, DV_A:]
    qk = _mm_nt(qs, ks) * gamma
    qd = qs * eg
    kd = ks * jnp.exp(glast - gc)
    s = s_ref[...]

    def diag_blocks(m):
        return jnp.concatenate(
            [m[h * c:(h + 1) * c, h * DV_A:(h + 1) * DV_A] for h in range(H_A)], axis=0)

    v_new = u - diag_blocks(_mm(w, s))
    o = diag_blocks(_mm(qd, s)) + _mm(qk, v_new)
    rblk = _iota((n, H_A * DV_A), 0) // c
    cblk = _iota((n, H_A * DV_A), 1) // DV_A
    v_bd = jnp.where(rblk == cblk, jnp.concatenate([v_new] * H_A, axis=1), 0.0)
    gl_row = jnp.concatenate(
        [jnp.broadcast_to(gcum[c - 1:c, 4 + h:5 + h], (1, DV_A)) for h in range(H_A)], axis=1)
    s_ref[...] = s * jnp.exp(gl_row) + _mm_tn(kd, v_bd)
    return jnp.concatenate([o[h * c:(h + 1) * c] for h in range(H_A)], axis=1)


def _hgrn_chunk(q, k, v, lf, st_ref):
    c = CHUNK
    d = D_C
    tri = (_iota((c, c), 1) <= _iota((c, c), 0)).astype(F32)
    bc = _mmh(tri, lf)
    row = _iota((c, d), 0)
    lane_head = _iota((c, d), 1) // DK_C
    srow = _iota((H_C * c, c), 0) % c
    scol = _iota((H_C * c, c), 1)

    def stack_masked(x):
        return jnp.concatenate([jnp.where(lane_head == h, x, 0.0) for h in range(H_C)], axis=0)

    att = jnp.zeros((H_C * c, c), F32)
    for s in (32, 16, 8):
        ref = jnp.concatenate(
            [jnp.broadcast_to(bc[r:r + 1, :], (2 * s, d)) for r in range(s, c, 2 * s)], axis=0)
        odd = ((row // s) % 2) == 1
        e = jnp.exp(jnp.where(odd, bc - ref, ref - bc))
        qt = jnp.where(odd, q * e, 0.0)
        kt = jnp.where(odd, 0.0, k * e)
        lvl = _mmh_nt(stack_masked(qt), kt)
        att = att + jnp.where((srow // (2 * s)) == (scol // (2 * s)), lvl, 0.0)
    prods = []
    for o in range(SUBLANES):
        ko = k if o == 0 else pltpu.roll(k, o, 0)
        bo = bc if o == 0 else pltpu.roll(bc, o, 0)
        valid = (row % SUBLANES) >= o
        prods.append(q * ko * jnp.exp(jnp.where(valid, bc - bo, -jnp.inf)))
    head_sel = (_iota((d, LANES), 0) // DK_C == _iota((d, LANES), 1)).astype(F32)
    dsum = _mmh(jnp.concatenate(prods, axis=0), head_sel)
    for o in range(SUBLANES):
        do = dsum[o * c:(o + 1) * c]
        dcol = jnp.concatenate([do[:, h:h + 1] for h in range(H_C)], axis=0)
        att = att + jnp.where(scol == srow - o, dcol, 0.0)
    o_all = _mm(att, v)
    o_intra = jnp.zeros((c, d), F32)
    for h in range(H_C):
        o_intra = o_intra + jnp.where(lane_head == h, o_all[h * c:(h + 1) * c], 0.0)
    st = st_ref[...]
    o_inter = _mm_nt(q * jnp.exp(bc), st)
    blast = bc[c - 1:c, :]
    kdec = k * jnp.exp(blast - bc)
    bd = (_iota((d, d), 0) // DK_C) == (_iota((d, d), 1) // DK_C)
    st_ref[...] = st * jnp.exp(blast) + jnp.where(bd, _mm_tn(v, kdec), 0.0)
    return o_intra + o_inter


def _mix_kernel(layer, x_ref, aconv_ref, as_ref, bconv_ref, bh_ref, cs_ref,
                gmix_ref, win_ref, aconvw_ref, ahead_ref, anorm_ref,
                bconvw_ref, bconvb_ref, wr_ref, br_ref, wi_ref, bi_ref, blam_ref,
                clb_ref, cnorm_ref, wout_ref,
                y_ref, aconv_o, as_o, bconv_o, bh_o, cs_o,
                proj, cat, aconv_s, sa_s, bconv_s, bh_s, sc_s):
    t = pl.program_id(1)
    nt = pl.num_programs(1)
    tile = x_ref.shape[1]
    d = D_C
    bd_c = (_iota((d, d), 0) // DK_C) == (_iota((d, d), 1) // DK_C)

    @pl.when(t == 0)
    def _load_state():
        aconv_s[...] = aconv_ref[0]
        bconv_s[...] = bconv_ref[0]
        bh_s[...] = bh_ref[0]
        for h in range(H_A):
            sa_s[:, h * DV_A:(h + 1) * DV_A] = as_ref[0, h]
        rows = jnp.concatenate([cs_ref[0, h] for h in range(H_C)], axis=0)
        tile_sel = (_iota((DK_C, d), 1) % DK_C == _iota((DK_C, d), 0)).astype(F32)
        s_bd = jnp.where(bd_c, _mmh(rows, tile_sel), 0.0)
        sc_s[...] = s_bd.T

    x = x_ref[0]
    h_in = _rms(x, gmix_ref[...]).astype(BF16)
    for j in range(0, N_PROJ, 256):
        wdt = min(256, N_PROJ - j)
        proj[:, j:j + wdt] = jnp.dot(h_in, win_ref[:, j:j + wdt], preferred_element_type=F32)

    pre = proj[:, OFF_QKV:OFF_QKV + 3 * D_A]
    qkv = _silu(_causal_conv(pre, aconv_s[...], aconvw_ref[...]))
    aconv_s[...] = pre[tile - SUBLANES:tile]
    for h in range(2 * H_A):
        blk = qkv[:, h * DK_A:(h + 1) * DK_A]
        nrm = blk * lax.rsqrt(jnp.sum(blk * blk, axis=-1, keepdims=True) + 1e-6)
        if h < H_A:
            nrm = nrm * (DK_A ** -0.5)
        proj[:, h * DK_A:(h + 1) * DK_A] = nrm
    proj[:, 2 * D_A:3 * D_A] = qkv[:, 2 * D_A:3 * D_A]
    ba = proj[:, OFF_BA:OFF_BA + LANES]
    proj[:, OFF_G:OFF_G + LANES] = (-jnp.exp(ahead_ref[0:1, :])) * jax.nn.softplus(ba + ahead_ref[1:2, :])
    proj[:, OFF_BA:OFF_BA + LANES] = jax.nn.sigmoid(ba)

    bpre = proj[:, OFF_BX:OFF_BX + D_B]
    xb = _causal_conv(bpre, bconv_s[...], bconvw_ref[...]) + bconvb_ref[...]
    bconv_s[...] = bpre[tile - SUBLANES:tile]
    r_gate = jax.nn.sigmoid(_mm(xb, wr_ref[...]) + br_ref[...])
    i_gate = jax.nn.sigmoid(_mm(xb, wi_ref[...]) + bi_ref[...])
    log_a = (-RG_C) * r_gate * jax.nn.softplus(-blam_ref[...])
    a_gate = jnp.exp(log_a)
    mult = jnp.sqrt(jnp.maximum(-jnp.tanh(log_a) * (a_gate * a_gate + 1.0), 0.0))
    a_cum, h_loc = _lin_scan(a_gate, mult * i_gate * xb)
    hb = a_cum * bh_s[...] + h_loc
    bh_s[...] = hb[tile - 1:tile]
    cat[:, D_A:D_A + D_B] = hb * jax.nn.gelu(proj[:, OFF_BG:OFF_BG + D_B])

    lg = clb_ref[...]
    ex = jnp.exp(lg - jnp.max(lg, axis=0, keepdims=True))
    sm = ex / jnp.sum(ex, axis=0, keepdims=True)
    lb = jnp.sum(sm[0:layer + 1], axis=0, keepdims=True) - sm[0:1]
    forget = lb + (1.0 - lb) * jax.nn.sigmoid(proj[:, OFF_CF:OFF_CF + d])
    proj[:, OFF_CQ:OFF_CQ + d] = _silu(proj[:, OFF_CQ:OFF_CQ + d])
    proj[:, OFF_CK:OFF_CK + d] = 1.0 - forget
    proj[:, OFF_CF:OFF_CF + d] = jnp.log(forget)

    def chunk_body(ci, carry):
        r0 = pl.multiple_of(ci * CHUNK, CHUNK)
        rows = pl.ds(r0, CHUNK)
        o_a = _gdn_chunk(proj[rows, 0:D_A], proj[rows, D_A:2 * D_A], proj[rows, 2 * D_A:3 * D_A],
                         proj[rows, OFF_BA:OFF_BA + LANES], proj[rows, OFF_G:OFF_G + LANES], sa_s)
        cat[rows, 0:D_A] = o_a
        o_c = _hgrn_chunk(proj[rows, OFF_CQ:OFF_CQ + d], proj[rows, OFF_CK:OFF_CK + d],
                          proj[rows, OFF_CI:OFF_CI + d], proj[rows, OFF_CF:OFF_CF + d], sc_s)
        cat[rows, D_A + D_B:D_MODEL] = o_c
        return carry

    lax.fori_loop(0, tile // CHUNK, chunk_body, 0)

    for h in range(H_A):
        blk = cat[:, h * DV_A:(h + 1) * DV_A]
        ms = jnp.mean(blk * blk, axis=-1, keepdims=True)
        z = proj[:, OFF_Z + h * DV_A:OFF_Z + (h + 1) * DV_A]
        cat[:, h * DV_A:(h + 1) * DV_A] = blk * lax.rsqrt(ms + RMS_EPS) * anorm_ref[...] * _silu(z)
    oc = cat[:, D_A + D_B:D_MODEL]
    ms_c = _mmh(oc * oc, jnp.where(bd_c, 1.0 / DK_C, 0.0))
    cat[:, D_A + D_B:D_MODEL] = (oc * lax.rsqrt(ms_c + RMS_EPS) * cnorm_ref[...]
                                 * _silu(proj[:, OFF_CG:OFF_CG + d]))
    y_ref[0] = x + _mm(cat[...], wout_ref[...])

    @pl.when(t == nt - 1)
    def _store_state():
        aconv_o[0] = aconv_s[...]
        bconv_o[0] = bconv_s[...]
        bh_o[0] = bh_s[...]
        for h in range(H_A):
            as_o[0, h] = sa_s[:, h * DV_A:(h + 1) * DV_A]
        s_bd = sc_s[...].T
        for h in range(H_C):
            sel = (_iota((d, DK_C), 0) == _iota((d, DK_C), 1) + h * DK_C).astype(F32)
            cs_o[0, h] = _mmh(s_bd[h * DK_C:(h + 1) * DK_C, :], sel)


def _ffn_kernel(final, x_ref, mk_ref, mv_ref, fconv_ref, gattn_ref, wq_ref, wo_ref,
                gffn_ref, wup_ref, fconvw_ref, fconvb_ref, wdown_ref, gfin_ref,
                y_ref, fconv_o, kb_s, vb_s, carry_s, act_s):
    t = pl.program_id(1)
    nt = pl.num_programs(1)
    tile = x_ref.shape[1]

    @pl.when(t == 0)
    def _load():
        kb_s[...] = mk_ref[0].astype(BF16)
        vb_s[...] = mv_ref[0].astype(BF16)
        carry_s[...] = fconv_ref[0]

    x = x_ref[0]
    hq = _rms(x, gattn_ref[...]).astype(BF16)
    q = jnp.dot(hq, wq_ref[...], preferred_element_type=F32)
    heads = []
    for h in range(MEM_HEADS):
        sl = slice(h * MEM_HEAD_DIM, (h + 1) * MEM_HEAD_DIM)
        s = _mm_nt(q[:, sl], kb_s[:, sl]) * (MEM_HEAD_DIM ** -0.5)
        e = jnp.exp(s - jnp.max(s, axis=-1, keepdims=True))
        p = e / jnp.sum(e, axis=-1, keepdims=True)
        heads.append(_mm(p, vb_s[:, sl]))
    x1 = x + _mm(jnp.concatenate(heads, axis=1), wo_ref[...])

    hf = _rms(x1, gffn_ref[...]).astype(BF16)
    for j in range(N_FFN_BLK):
        halves = []
        for off in (j * FFN_BLK, D_FF + j * FFN_BLK):
            u = jnp.dot(hf, wup_ref[:, off:off + FFN_BLK], preferred_element_type=F32)
            halves.append(_causal_conv(u, carry_s[:, off:off + FFN_BLK], fconvw_ref[:, off:off + FFN_BLK])
                          + fconvb_ref[:, off:off + FFN_BLK])
            carry_s[:, off:off + FFN_BLK] = u[tile - SUBLANES:tile]
        act_s[:, j * FFN_BLK:(j + 1) * FFN_BLK] = (_silu(halves[0]) * halves[1]).astype(BF16)
    y = x1 + jnp.dot(act_s[...], wdown_ref[...], preferred_element_type=F32)
    if final:
        y = _rms(y, gfin_ref[...])
    y_ref[0] = y

    @pl.when(t == nt - 1)
    def _store():
        fconv_o[0] = carry_s[...]


def _memkv_kernel(mem_ref, g_ref, wk_ref, wv_ref, k_ref, v_ref):
    mn = _rms(mem_ref[0], g_ref[0]).astype(BF16)
    k_ref[0, 0] = jnp.dot(mn, wk_ref[0], preferred_element_type=F32)
    v_ref[0, 0] = jnp.dot(mn, wv_ref[0], preferred_element_type=F32)


def _resident(shape):
    nd = len(shape)
    return pl.BlockSpec(shape, lambda b, t: (0,) * nd, pipeline_mode=pl.Buffered(1))


def _per_batch(shape):
    nd = len(shape)
    return pl.BlockSpec((1,) + tuple(shape[1:]), lambda b, t: (b,) + (0,) * (nd - 1))


def _seq_tile(length):
    return 256 if length % 256 == 0 else CHUNK


def _mix_call(layer, x, states, wts):
    bsz, length, _ = x.shape
    tile = _seq_tile(length)
    x_spec = pl.BlockSpec((1, tile, D_MODEL), lambda b, t: (b, t, 0))
    state_specs = [_per_batch(s.shape) for s in states]
    out_shapes = [jax.ShapeDtypeStruct(x.shape, F32)] + [jax.ShapeDtypeStruct(s.shape, F32) for s in states]
    return pl.pallas_call(
        functools.partial(_mix_kernel, layer),
        grid=(bsz, length // tile),
        in_specs=[x_spec] + state_specs + [_resident(w.shape) for w in wts],
        out_specs=[x_spec] + state_specs,
        out_shape=out_shapes,
        scratch_shapes=[
            pltpu.VMEM((tile, PROJ_W), F32),
            pltpu.VMEM((tile, D_MODEL), F32),
            pltpu.VMEM((SUBLANES, 3 * D_A), F32),
            pltpu.VMEM((DK_A, H_A * DV_A), F32),
            pltpu.VMEM((SUBLANES, D_B), F32),
            pltpu.VMEM((1, D_B), F32),
            pltpu.VMEM((D_C, D_C), F32),
        ],
        compiler_params=pltpu.CompilerParams(
            dimension_semantics=("arbitrary", "arbitrary"), vmem_limit_bytes=VMEM_LIMIT_BYTES),
        name="mix_block",
    )(x, *states, *wts)


def _ffn_call(final, x, mem_k, mem_v, fconv, wts):
    bsz, length, _ = x.shape
    tile = _seq_tile(length)
    x_spec = pl.BlockSpec((1, tile, D_MODEL), lambda b, t: (b, t, 0))
    return pl.pallas_call(
        functools.partial(_ffn_kernel, final),
        grid=(bsz, length // tile),
        in_specs=[x_spec, _per_batch(mem_k.shape), _per_batch(mem_v.shape), _per_batch(fconv.shape)]
        + [_resident(w.shape) for w in wts],
        out_specs=[x_spec, _per_batch(fconv.shape)],
        out_shape=[jax.ShapeDtypeStruct(x.shape, F32), jax.ShapeDtypeStruct(fconv.shape, F32)],
        scratch_shapes=[
            pltpu.VMEM((N_MEM, D_MODEL), BF16),
            pltpu.VMEM((N_MEM, D_MODEL), BF16),
            pltpu.VMEM((SUBLANES, 2 * D_FF), F32),
            pltpu.VMEM((tile, D_FF), BF16),
        ],
        compiler_params=pltpu.CompilerParams(
            dimension_semantics=("arbitrary", "arbitrary"), vmem_limit_bytes=VMEM_LIMIT_BYTES),
        name="attn_ffn_block",
    )(x, mem_k, mem_v, fconv, *wts)


def _memkv_call(mem, norm_mem, wk, wv):
    depth = wk.shape[0]
    bsz, n_mem, _ = mem.shape
    w_spec = pl.BlockSpec((1, D_MODEL, D_MODEL), lambda l, b: (l, 0, 0))
    o_spec = pl.BlockSpec((1, 1, n_mem, D_MODEL), lambda l, b: (l, b, 0, 0))
    shape = jax.ShapeDtypeStruct((depth, bsz, n_mem, D_MODEL), F32)
    return pl.pallas_call(
        _memkv_kernel,
        grid=(depth, bsz),
        in_specs=[pl.BlockSpec((1, n_mem, D_MODEL), lambda l, b: (b, 0, 0)),
                  pl.BlockSpec((1, 1, D_MODEL), lambda l, b: (l, 0, 0)), w_spec, w_spec],
        out_specs=[o_spec, o_spec],
        out_shape=[shape, shape],
        compiler_params=pltpu.CompilerParams(
            dimension_semantics=("arbitrary", "arbitrary"), vmem_limit_bytes=VMEM_LIMIT_BYTES),
        name="memory_kv",
    )(mem, norm_mem.reshape(depth, 1, D_MODEL), wk, wv)


def _pad_rows8(s):
    return jnp.pad(s, ((0, 0), (SUBLANES - s.shape[1], 0), (0, 0)))


def _block_diag(w):
    n, bw, _ = w.shape
    out = jnp.zeros((n * bw, n * bw), w.dtype)
    for i in range(n):
        out = out.at[i * bw:(i + 1) * bw, i * bw:(i + 1) * bw].set(w[i])
    return out


def _lane_pad(v, offset):
    return jnp.zeros((LANES,), F32).at[offset:offset + v.shape[0]].set(v)


def kernel(x_prompt, x_sample, cache_mem_k, cache_mem_v, state_a_conv, state_a_S, state_b_conv, state_b_h, state_c_S, state_ffn_conv, mem_prompt, norm_mix, w_in, a_conv_w, a_A_log, a_dt_bias, a_norm, b_conv_w, b_conv_b, b_w_r, b_b_r, b_w_i, b_b_i, b_lambda, c_lb_logits, c_norm, w_out, norm_attn, norm_mem, w_q, w_k, w_v, w_o, norm_ffn, w_up, ffn_conv_w, ffn_conv_b, w_down, norm_final):
    depth = w_in.shape[0]
    bsz = x_prompt.shape[0]
    dbsz = x_sample.shape[0]

    def row(v):
        return v.reshape(1, -1)

    mix_w, ffn_w = [], []
    for l in range(depth):
        w = w_in[l]
        pad = jnp.zeros((D_MODEL, LANES - 2 * H_A), F32)
        w_pad = jnp.concatenate([w[:, :OFF_BA], w[:, OFF_BA:OFF_BA + 2 * H_A], pad, w[:, OFF_BA + 2 * H_A:]],
                                axis=1).astype(BF16)
        a_head = jnp.zeros((SUBLANES, LANES), F32)
        a_head = a_head.at[0].set(_lane_pad(a_A_log[l], H_A)).at[1].set(_lane_pad(a_dt_bias[l], H_A))
        mix_w.append([
            row(norm_mix[l]), w_pad, a_conv_w[l], a_head, row(a_norm[l]),
            b_conv_w[l], row(b_conv_b[l]), _block_diag(b_w_r[l]).astype(BF16), row(b_b_r[l]),
            _block_diag(b_w_i[l]).astype(BF16), row(b_b_i[l]), row(b_lambda[l]),
            c_lb_logits, row(jnp.tile(c_norm[l], H_C)), w_out[l].astype(BF16)])
        ffn_w.append([
            row(norm_attn[l]), w_q[l].astype(BF16), w_o[l].astype(BF16), row(norm_ffn[l]),
            w_up[l].astype(BF16), ffn_conv_w[l], row(ffn_conv_b[l]), w_down[l].astype(BF16),
            row(norm_final)])

    p_mem_k, p_mem_v = _memkv_call(mem_prompt, norm_mem, w_k.astype(BF16), w_v.astype(BF16))

    def run_group(x, mem_k, mem_v, a_conv, a_s, b_conv, b_h, c_s, f_conv):
        outs = []
        for l in range(depth):
            states = [_pad_rows8(a_conv[l]), a_s[l], _pad_rows8(b_conv[l]), b_h[l][:, None, :], c_s[l]]
            x, na_conv, na_s, nb_conv, nb_h, nc_s = _mix_call(l, x, states, mix_w[l])
            x, nf_conv = _ffn_call(l == depth - 1, x, mem_k[l], mem_v[l], _pad_rows8(f_conv[l]), ffn_w[l])
            outs.append((na_conv[:, SUBLANES - (A_CONV - 1):], na_s, nb_conv[:, SUBLANES - (B_CONV - 1):],
                         nb_h[:, 0], nc_s, nf_conv[:, SUBLANES - (FFN_CONV - 1):]))
        return x, [jnp.stack(t, axis=0) for t in zip(*outs)]

    def zeros(n, *shape):
        return jnp.zeros((depth, n) + shape, F32)

    y_prompt, p_states = run_group(
        x_prompt, p_mem_k, p_mem_v,
        zeros(bsz, A_CONV - 1, 3 * D_A), zeros(bsz, H_A, DK_A, DV_A), zeros(bsz, B_CONV - 1, D_B),
        zeros(bsz, D_B), zeros(bsz, H_C, DK_C, DK_C), zeros(bsz, FFN_CONV - 1, 2 * D_FF))
    y_sample, s_states = run_group(
        x_sample, cache_mem_k.reshape(depth, dbsz, N_MEM, D_MODEL), cache_mem_v.reshape(depth, dbsz, N_MEM, D_MODEL),
        state_a_conv, state_a_S, state_b_conv, state_b_h, state_c_S, state_ffn_conv)

    kv_shape = (depth, bsz, N_MEM, MEM_HEADS, MEM_HEAD_DIM)
    return (y_prompt, y_sample, p_mem_k.reshape(kv_shape), p_mem_v.reshape(kv_shape),
            *p_states, *s_states)
```

```python
import functools
import math

import jax
import jax.numpy as jnp
from jax import lax
from jax.experimental import pallas as pl
from jax.experimental.pallas import tpu as pltpu

F32 = jnp.float32
BF16 = jnp.bfloat16

D_MODEL = 1024
CHUNK = 64
D_A = 512
DK_A = 128
DV_A = 128
H_A = 4
A_CONV = 4
D_B = 256
B_BLOCKS = 4
B_CONV = 4
RG_C = 8.0
D_C = 256
H_C = 4
DK_C = 64
N_MEM = 256
MEM_HEADS = 4
MEM_HEAD_DIM = 256
D_FF = 2816
FFN_CONV = 3
RMS_EPS = 1e-6

LANES = 128
SUBLANES = 8
VMEM_LIMIT_BYTES = 56 * 1024 * 1024

OFF_QKV = 0
OFF_Z = 1536
OFF_BA = 2048
OFF_BX = 2176
OFF_BG = 2432
OFF_CQ = 2688
OFF_CF = 2944
OFF_CI = 3200
OFF_CG = 3456
N_PROJ = 3712
OFF_CK = 3712
OFF_G = 3968
PROJ_W = 4096
FFN_BLK = 256
N_FFN_BLK = D_FF // FFN_BLK

_NT = (((1,), (1,)), ((), ()))
_TN = (((0,), (0,)), ((), ()))


def _mm(a, b):
    return jnp.dot(a.astype(BF16), b.astype(BF16), preferred_element_type=F32)


def _mm_nt(a, b):
    return lax.dot_general(a.astype(BF16), b.astype(BF16), _NT, preferred_element_type=F32)


def _split3(x):
    hi = x.astype(BF16)
    r = x - hi.astype(F32)
    mid = r.astype(BF16)
    lo = (r - mid.astype(F32)).astype(BF16)
    return hi, mid, lo


def _mm_sel(x, sel, parts=3):
    return sum(jnp.dot(part, sel, preferred_element_type=F32) for part in _split3(x)[:parts])


def _cumsum_rows(x):
    c = x.shape[0]
    tri3 = ((_iota((c, 3 * c), 1) % c) <= _iota((c, 3 * c), 0)).astype(BF16)
    return jnp.dot(tri3, jnp.concatenate(_split3(x), axis=0), preferred_element_type=F32)


def _rms(x, g):
    ms = jnp.mean(x * x, axis=-1, keepdims=True)
    return x * lax.rsqrt(ms + RMS_EPS) * g


def _silu(x):
    return x * jax.nn.sigmoid(x)


def _iota(shape, axis):
    return lax.broadcasted_iota(jnp.int32, shape, axis)


def _shift_rows(x, carry8, s):
    rolled = pltpu.roll(x, s, 0)
    head = jnp.where(_iota(carry8.shape, 0) < s, pltpu.roll(carry8, s, 0), rolled[0:SUBLANES])
    if x.shape[0] == SUBLANES:
        return head
    return jnp.concatenate([head, rolled[SUBLANES:]], axis=0)


def _causal_conv(x, carry8, w):
    width = w.shape[0]
    y = x * w[width - 1:width, :]
    for s in range(1, width):
        y = y + _shift_rows(x, carry8, s) * w[width - 1 - s:width - s, :]
    return y


def _lin_scan(a, u):
    n = a.shape[0]
    row = _iota(a.shape, 0)
    s = 1
    while s < n:
        keep = row >= s
        a_s = jnp.where(keep, pltpu.roll(a, s, 0), 1.0)
        u_s = jnp.where(keep, pltpu.roll(u, s, 0), 0.0)
        u = a * u_s + u
        a = a * a_s
        s *= 2
    return a, u


def _tile_rows_masked(x, mask):
    return jnp.concatenate([x] * (mask.shape[0] // x.shape[0]), axis=0) * mask


def _chunk_masks():
    c = CHUNK
    n = H_A * c
    rowi = _iota((c, n), 0)
    lane_j = _iota((c, n), 1) % c
    m = {
        "half": _iota((c, LANES), 1) < c,
        "incl": lane_j <= rowi,
        "strict": lane_j < rowi,
        "eye": jnp.where(lane_j == rowi, 1.0, 0.0),
        "bd_sq": jnp.where(_iota((n, n), 0) // c == _iota((n, n), 1) // c, 1.0, 0.0).astype(BF16),
        "bd_wide": jnp.where(_iota((n, H_A * DV_A), 0) // c == _iota((n, H_A * DV_A), 1) // DV_A,
                             1.0, 0.0).astype(BF16),
        "ones3": jnp.ones((c, 3 * c), BF16),
    }
    for s in (32, 16, 8):
        m["odd", s] = ((_iota((c, D_C), 0) // s) % 2) == 1
        m["pair", s] = (rowi // (2 * s)) == (lane_j // (2 * s))
    for o in range(SUBLANES):
        m["valid", o] = (_iota((c, D_C), 0) % SUBLANES) >= o
        m["diag", o] = lane_j == rowi - o
    return m


def _each(fn, *lists):
    return [fn(*args) for args in zip(*lists)]


def _gdn_prep(qs, ks, vs, beta_blks, g_blks, m):
    c = CHUNK
    dot = functools.partial(jnp.dot, preferred_element_type=F32)
    gcum = _each(_cumsum_rows, g_blks)

    def lane_forms(blk, first):
        cols = [jnp.broadcast_to(blk[:, first + h:first + h + 1], (c, LANES)) for h in range(H_A)]
        wide = jnp.concatenate(cols, axis=1)
        sq = jnp.concatenate([jnp.where(m["half"], cols[0], cols[1]), jnp.where(m["half"], cols[2], cols[3])],
                             axis=1)
        return wide, sq

    g_forms = _each(lambda g: lane_forms(g, H_A), gcum)
    b_forms = _each(lambda b: lane_forms(b, 0), beta_blks)
    gr = _each(lambda gf: dot(m["ones3"], jnp.concatenate(_split3(gf[1] * m["eye"]), axis=0)), g_forms)
    gamma = _each(lambda gf, r: jnp.exp(jnp.where(m["incl"], gf[1] - r, -jnp.inf)), g_forms, gr)
    kb = _each(lambda k: k.astype(BF16), ks)
    qk_kk = _each(lambda q, k_b: lax.dot_general(jnp.concatenate([q.astype(BF16), k_b], axis=0),
                                                 _tile_rows_masked(k_b, m["bd_wide"]), _NT,
                                                 preferred_element_type=F32), qs, kb)
    qk = _each(lambda r, gm: (r[:c] * gm).astype(BF16), qk_kk, gamma)
    a = _each(lambda bf, r, gm: jnp.where(m["strict"], bf[1] * r[c:] * gm, 0.0), b_forms, qk_kk, gamma)
    x = _each(lambda a_: m["eye"] - a_, a)
    ab = _each(lambda a_: a_.astype(BF16), a)
    p = _each(lambda a_b: dot(a_b, _tile_rows_masked(a_b, m["bd_sq"])), ab)
    for it in range(5):
        pb = _each(lambda p_: p_.astype(BF16), p)
        p_bd = _each(lambda p_b: _tile_rows_masked(p_b, m["bd_sq"]), pb)
        if it < 4:
            both = _each(lambda x_, p_b, bd: dot(jnp.concatenate([x_.astype(BF16), p_b], axis=0), bd), x, pb, p_bd)
            x = _each(lambda x_, r: x_ + r[:c], x, both)
            p = _each(lambda r: r[c:], both)
        else:
            x = _each(lambda x_, bd: x_ + dot(x_.astype(BF16), bd), x, p_bd)
    xb = _each(lambda x_: x_.astype(BF16), x)
    eg = _each(lambda gf: jnp.exp(gf[0]), g_forms)
    u = _each(lambda x_b, v, bf: dot(x_b, _tile_rows_masked((v * bf[0]).astype(BF16), m["bd_wide"])),
              xb, vs, b_forms)
    w = _each(lambda x_b, k, bf, e: dot(x_b, _tile_rows_masked((k * (bf[0] * e)).astype(BF16), m["bd_wide"])),
              xb, ks, b_forms, eg)
    glast = _each(lambda g: jnp.concatenate(
        [jnp.broadcast_to(g[c - 1:c, H_A + h:H_A + h + 1], (1, DV_A)) for h in range(H_A)], axis=1), gcum)
    kd = _each(lambda k, gl, gf: (k * jnp.exp(gl - gf[0])).astype(BF16), ks, glast, g_forms)
    kd_stack = _each(lambda kd_: jnp.concatenate([kd_[:, h * DK_A:(h + 1) * DK_A] for h in range(H_A)], axis=0), kd)
    wq = _each(lambda w_, q, e: [
        jnp.concatenate([w_[:, h * DK_A:(h + 1) * DK_A], (q * e)[:, h * DK_A:(h + 1) * DK_A]], axis=0).astype(BF16)
        for h in range(H_A)], w, qs, eg)
    decay = _each(jnp.exp, glast)
    return list(zip(u, wq, qk, kd_stack, decay))


def _gdn_step(prep, s_ref, m):
    u, wq, qk, kd_stack, decay = prep
    c = CHUNK
    s = s_ref[...]
    sb = s.astype(BF16)
    v_new, o_state = [], []
    for h in range(H_A):
        hs = slice(h * DV_A, (h + 1) * DV_A)
        ws = jnp.dot(wq[h], sb[:, hs], preferred_element_type=F32)
        v_new.append(u[:, hs] - ws[:c])
        o_state.append(ws[c:])
    v_bd = _tile_rows_masked(jnp.concatenate(v_new, axis=1).astype(BF16), m["bd_wide"])
    o = jnp.concatenate(o_state, axis=1) + jnp.dot(qk, v_bd, preferred_element_type=F32)
    s_ref[...] = s * decay + lax.dot_general(kd_stack, v_bd, _TN, preferred_element_type=F32)
    return o


def _hgrn_prep(qs, ks, vs, lfs, m):
    c = CHUNK
    d = D_C
    dot = functools.partial(jnp.dot, preferred_element_type=F32)
    bc = _each(_cumsum_rows, lfs)
    att = [jnp.zeros((c, H_C * c), F32) for _ in qs]
    for s in (32, 16, 8):
        odd = m["odd", s]
        ref = _each(lambda b: jnp.concatenate(
            [jnp.broadcast_to(b[r:r + 1, :], (2 * s, d)) for r in range(s, c, 2 * s)], axis=0), bc)
        e = _each(lambda b, rf: jnp.exp(jnp.where(odd, b - rf, rf - b)), bc, ref)
        qt = _each(lambda q, e_: jnp.where(odd, q * e_, 0.0).astype(BF16), qs, e)
        kt = _each(lambda k, e_: jnp.where(odd, 0.0, k * e_).astype(BF16), ks, e)
        lvl = _each(lambda q_t, k_t: lax.dot_general(q_t, _tile_rows_masked(k_t, m["bd_sq"]), _NT,
                                                     preferred_element_type=F32), qt, kt)
        att = _each(lambda a_, l: a_ + jnp.where(m["pair", s], l, 0.0), att, lvl)

    def shifted_prods(q, k, b):
        out = []
        for o in range(SUBLANES):
            ko = k if o == 0 else pltpu.roll(k, o, 0)
            bo = b if o == 0 else pltpu.roll(b, o, 0)
            out.append((q * ko * jnp.exp(jnp.where(m["valid", o], b - bo, -jnp.inf))).astype(BF16))
        return jnp.concatenate(out, axis=0)

    prods = _each(shifted_prods, qs, ks, bc)
    dsum = _each(lambda pr: dot(pr, m["bd_sq"]), prods)
    for o in range(SUBLANES):
        att = _each(lambda a_, ds: a_ + jnp.where(m["diag", o], ds[o * c:(o + 1) * c], 0.0), att, dsum)
    vb = _each(lambda v: v.astype(BF16), vs)
    o_intra = _each(lambda a_, v_b: dot(a_.astype(BF16), _tile_rows_masked(v_b, m["bd_sq"])), att, vb)
    qdec = _each(lambda q, b: (q * jnp.exp(b)).astype(BF16), qs, bc)
    kdec = _each(lambda k, b: (k * jnp.exp(b[c - 1:c, :] - b)).astype(BF16), ks, bc)
    decay = _each(lambda b: jnp.exp(b[c - 1:c, :]), bc)
    return list(zip(o_intra, qdec, kdec, vb, decay))


def _hgrn_step(prep, st_ref, m):
    o_intra, qdec, kdec, vb, decay = prep
    st = st_ref[...]
    o = o_intra + lax.dot_general(qdec, st.astype(BF16), _NT, preferred_element_type=F32)
    upd = lax.dot_general(vb, kdec, _TN, preferred_element_type=F32)
    st_ref[...] = st * decay + upd * m["bd_sq"].astype(F32)
    return o


def _mix_kernel(layer, x_ref, aconv_ref, as_ref, bconv_ref, bh_ref, cs_ref,
                gmix_ref, win_ref, aconvw_ref, ahead_ref, anorm_ref,
                bconvw_ref, bconvb_ref, wr_ref, br_ref, wi_ref, bi_ref, blam_ref,
                clb_ref, cnorm_ref, wout_ref,
                y_ref, aconv_o, as_o, bconv_o, bh_o, cs_o,
                proj, cat, aconv_s, sa_s, bconv_s, bh_s, sc_s):
    t = pl.program_id(1)
    nt = pl.num_programs(1)
    tile = x_ref.shape[1]
    d = D_C
    masks = _chunk_masks()

    @pl.when(t == 0)
    def _load_state():
        aconv_s[...] = aconv_ref[0]
        bconv_s[...] = bconv_ref[0]
        bh_s[...] = bh_ref[0]
        for h in range(H_A):
            sa_s[:, h * DV_A:(h + 1) * DV_A] = as_ref[0, h]
        rows = jnp.concatenate([cs_ref[0, h] for h in range(H_C)], axis=0)
        tile_sel = jnp.where(_iota((DK_C, d), 1) % DK_C == _iota((DK_C, d), 0), 1.0, 0.0).astype(BF16)
        s_bd = _mm_sel(rows, tile_sel) * masks["bd_sq"].astype(F32)
        sc_s[...] = s_bd.T

    x = x_ref[0]
    h_in = _rms(x, gmix_ref[...]).astype(BF16)
    for j in range(0, N_PROJ, 256):
        wdt = min(256, N_PROJ - j)
        proj[:, j:j + wdt] = jnp.dot(h_in, win_ref[:, j:j + wdt], preferred_element_type=F32)

    pre = proj[:, OFF_QKV:OFF_QKV + 3 * D_A]
    qkv = _silu(_causal_conv(pre, aconv_s[...], aconvw_ref[...]))
    aconv_s[...] = pre[tile - SUBLANES:tile]
    for h in range(2 * H_A):
        blk = qkv[:, h * DK_A:(h + 1) * DK_A]
        nrm = blk * lax.rsqrt(jnp.sum(blk * blk, axis=-1, keepdims=True) + 1e-6)
        if h < H_A:
            nrm = nrm * (DK_A ** -0.5)
        proj[:, h * DK_A:(h + 1) * DK_A] = nrm
    proj[:, 2 * D_A:3 * D_A] = qkv[:, 2 * D_A:3 * D_A]
    ba = proj[:, OFF_BA:OFF_BA + LANES]
    proj[:, OFF_G:OFF_G + LANES] = (-jnp.exp(ahead_ref[0:1, :])) * jax.nn.softplus(ba + ahead_ref[1:2, :])
    proj[:, OFF_BA:OFF_BA + LANES] = jax.nn.sigmoid(ba)

    bpre = proj[:, OFF_BX:OFF_BX + D_B]
    xb = _causal_conv(bpre, bconv_s[...], bconvw_ref[...]) + bconvb_ref[...]
    bconv_s[...] = bpre[tile - SUBLANES:tile]
    r_gate = jax.nn.sigmoid(_mm(xb, wr_ref[...]) + br_ref[...])
    i_gate = jax.nn.sigmoid(_mm(xb, wi_ref[...]) + bi_ref[...])
    log_a = (-RG_C) * r_gate * jax.nn.softplus(-blam_ref[...])
    a_gate = jnp.exp(log_a)
    one_m_a2 = -jnp.tanh(log_a) * (a_gate * a_gate + 1.0)
    mult = jnp.where(one_m_a2 > 0.0, one_m_a2 * lax.rsqrt(one_m_a2), 0.0)
    a_cum, h_loc = _lin_scan(a_gate, mult * i_gate * xb)
    hb = a_cum * bh_s[...] + h_loc
    bh_s[...] = hb[tile - 1:tile]
    cat[:, D_A:D_A + D_B] = hb * jax.nn.gelu(proj[:, OFF_BG:OFF_BG + D_B])

    lg = clb_ref[...]
    ex = jnp.exp(lg - jnp.max(lg, axis=0, keepdims=True))
    sm = ex / jnp.sum(ex, axis=0, keepdims=True)
    lb = jnp.sum(sm[0:layer + 1], axis=0, keepdims=True) - sm[0:1]
    forget = lb + (1.0 - lb) * jax.nn.sigmoid(proj[:, OFF_CF:OFF_CF + d])
    proj[:, OFF_CQ:OFF_CQ + d] = _silu(proj[:, OFF_CQ:OFF_CQ + d])
    proj[:, OFF_CK:OFF_CK + d] = 1.0 - forget
    proj[:, OFF_CF:OFF_CF + d] = jnp.log(forget)

    chunks = [slice(ci * CHUNK, (ci + 1) * CHUNK) for ci in range(tile // CHUNK)]

    def cols(lo, width):
        return [proj[rows, lo:lo + width] for rows in chunks]

    gdn_preps = _gdn_prep(cols(0, D_A), cols(D_A, D_A), cols(2 * D_A, D_A), cols(OFF_BA, LANES),
                          cols(OFF_G, LANES), masks)
    hgrn_preps = _hgrn_prep(cols(OFF_CQ, d), cols(OFF_CK, d), cols(OFF_CI, d), cols(OFF_CF, d), masks)
    for rows, gdn_prep, hgrn_prep in zip(chunks, gdn_preps, hgrn_preps):
        cat[rows, 0:D_A] = _gdn_step(gdn_prep, sa_s, masks)
        cat[rows, D_A + D_B:D_MODEL] = _hgrn_step(hgrn_prep, sc_s, masks)

    for h in range(H_A):
        blk = cat[:, h * DV_A:(h + 1) * DV_A]
        ms = jnp.mean(blk * blk, axis=-1, keepdims=True)
        z = proj[:, OFF_Z + h * DV_A:OFF_Z + (h + 1) * DV_A]
        cat[:, h * DV_A:(h + 1) * DV_A] = blk * lax.rsqrt(ms + RMS_EPS) * anorm_ref[...] * _silu(z)
    oc = cat[:, D_A + D_B:D_MODEL]
    ms_c = _mm_sel(oc * oc, masks["bd_sq"], parts=2) * (1.0 / DK_C)
    cat[:, D_A + D_B:D_MODEL] = (oc * lax.rsqrt(ms_c + RMS_EPS) * cnorm_ref[...]
                                 * _silu(proj[:, OFF_CG:OFF_CG + d]))
    y_ref[0] = x + _mm(cat[...], wout_ref[...])

    @pl.when(t == nt - 1)
    def _store_state():
        aconv_o[0] = aconv_s[...]
        bconv_o[0] = bconv_s[...]
        bh_o[0] = bh_s[...]
        for h in range(H_A):
            as_o[0, h] = sa_s[:, h * DV_A:(h + 1) * DV_A]
        s_bd = sc_s[...].T
        for h in range(H_C):
            sel = jnp.where(_iota((d, DK_C), 0) == _iota((d, DK_C), 1) + h * DK_C, 1.0, 0.0).astype(BF16)
            cs_o[0, h] = _mm_sel(s_bd[h * DK_C:(h + 1) * DK_C, :], sel)


def _ffn_kernel(final, x_ref, mk_ref, mv_ref, fconv_ref, gattn_ref, wq_ref, wo_ref,
                gffn_ref, wup_ref, fconvw_ref, fconvb_ref, wdown_ref, gfin_ref,
                y_ref, fconv_o, kb_s, vb_s, carry_s, act_s):
    t = pl.program_id(1)
    nt = pl.num_programs(1)
    tile = x_ref.shape[1]

    @pl.when(t == 0)
    def _load():
        kb_s[...] = mk_ref[0].astype(BF16)
        vb_s[...] = mv_ref[0].astype(BF16)
        carry_s[...] = fconv_ref[0]

    x = x_ref[0]
    hq = _rms(x, gattn_ref[...]).astype(BF16)
    q = jnp.dot(hq, wq_ref[...], preferred_element_type=F32)
    heads = []
    for h in range(MEM_HEADS):
        sl = slice(h * MEM_HEAD_DIM, (h + 1) * MEM_HEAD_DIM)
        s = _mm_nt(q[:, sl], kb_s[:, sl]) * (MEM_HEAD_DIM ** -0.5)
        e = jnp.exp(s - jnp.max(s, axis=-1, keepdims=True))
        p = e / jnp.sum(e, axis=-1, keepdims=True)
        heads.append(_mm(p, vb_s[:, sl]))
    x1 = x + _mm(jnp.concatenate(heads, axis=1), wo_ref[...])

    hf = _rms(x1, gffn_ref[...]).astype(BF16)
    for j in range(N_FFN_BLK):
        halves = []
        for off in (j * FFN_BLK, D_FF + j * FFN_BLK):
            u = jnp.dot(hf, wup_ref[:, off:off + FFN_BLK], preferred_element_type=F32)
            halves.append(_causal_conv(u, carry_s[:, off:off + FFN_BLK], fconvw_ref[:, off:off + FFN_BLK])
                          + fconvb_ref[:, off:off + FFN_BLK])
            carry_s[:, off:off + FFN_BLK] = u[tile - SUBLANES:tile]
        act_s[:, j * FFN_BLK:(j + 1) * FFN_BLK] = (_silu(halves[0]) * halves[1]).astype(BF16)
    y = x1 + jnp.dot(act_s[...], wdown_ref[...], preferred_element_type=F32)
    if final:
        y = _rms(y, gfin_ref[...])
    y_ref[0] = y

    @pl.when(t == nt - 1)
    def _store():
        fconv_o[0] = carry_s[...]


def _memkv_kernel(mem_ref, g_ref, wk_ref, wv_ref, k_ref, v_ref):
    mn = _rms(mem_ref[0], g_ref[0]).astype(BF16)
    k_ref[0, 0] = jnp.dot(mn, wk_ref[0], preferred_element_type=F32)
    v_ref[0, 0] = jnp.dot(mn, wv_ref[0], preferred_element_type=F32)


def _resident(shape):
    nd = len(shape)
    return pl.BlockSpec(shape, lambda b, t: (0,) * nd, pipeline_mode=pl.Buffered(1))


def _per_batch(shape):
    nd = len(shape)
    return pl.BlockSpec((1,) + tuple(shape[1:]), lambda b, t: (b,) + (0,) * (nd - 1))


def _seq_tile(length):
    return 256 if length % 256 == 0 else CHUNK


def _mix_call(layer, x, states, wts):
    bsz, length, _ = x.shape
    tile = _seq_tile(length)
    x_spec = pl.BlockSpec((1, tile, D_MODEL), lambda b, t: (b, t, 0))
    state_specs = [_per_batch(s.shape) for s in states]
    out_shapes = [jax.ShapeDtypeStruct(x.shape, F32)] + [jax.ShapeDtypeStruct(s.shape, F32) for s in states]
    return pl.pallas_call(
        functools.partial(_mix_kernel, layer),
        grid=(bsz, length // tile),
        in_specs=[x_spec] + state_specs + [_resident(w.shape) for w in wts],
        out_specs=[x_spec] + state_specs,
        out_shape=out_shapes,
        scratch_shapes=[
            pltpu.VMEM((tile, PROJ_W), F32),
            pltpu.VMEM((tile, D_MODEL), F32),
            pltpu.VMEM((SUBLANES, 3 * D_A), F32),
            pltpu.VMEM((DK_A, H_A * DV_A), F32),
            pltpu.VMEM((SUBLANES, D_B), F32),
            pltpu.VMEM((1, D_B), F32),
            pltpu.VMEM((D_C, D_C), F32),
        ],
        compiler_params=pltpu.CompilerParams(
            dimension_semantics=("arbitrary", "arbitrary"), vmem_limit_bytes=VMEM_LIMIT_BYTES),
        name="mix_block",
    )(x, *states, *wts)


def _ffn_call(final, x, mem_k, mem_v, fconv, wts):
    bsz, length, _ = x.shape
    tile = _seq_tile(length)
    x_spec = pl.BlockSpec((1, tile, D_MODEL), lambda b, t: (b, t, 0))
    return pl.pallas_call(
        functools.partial(_ffn_kernel, final),
        grid=(bsz, length // tile),
        in_specs=[x_spec, _per_batch(mem_k.shape), _per_batch(mem_v.shape), _per_batch(fconv.shape)]
        + [_resident(w.shape) for w in wts],
        out_specs=[x_spec, _per_batch(fconv.shape)],
        out_shape=[jax.ShapeDtypeStruct(x.shape, F32), jax.ShapeDtypeStruct(fconv.shape, F32)],
        scratch_shapes=[
            pltpu.VMEM((N_MEM, D_MODEL), BF16),
            pltpu.VMEM((N_MEM, D_MODEL), BF16),
            pltpu.VMEM((SUBLANES, 2 * D_FF), F32),
            pltpu.VMEM((tile, D_FF), BF16),
        ],
        compiler_params=pltpu.CompilerParams(
            dimension_semantics=("arbitrary", "arbitrary"), vmem_limit_bytes=VMEM_LIMIT_BYTES),
        name="attn_ffn_block",
    )(x, mem_k, mem_v, fconv, *wts)


def _memkv_call(mem, norm_mem, wk, wv):
    depth = wk.shape[0]
    bsz, n_mem, _ = mem.shape
    w_spec = pl.BlockSpec((1, D_MODEL, D_MODEL), lambda l, b: (l, 0, 0))
    o_spec = pl.BlockSpec((1, 1, n_mem, D_MODEL), lambda l, b: (l, b, 0, 0))
    shape = jax.ShapeDtypeStruct((depth, bsz, n_mem, D_MODEL), F32)
    return pl.pallas_call(
        _memkv_kernel,
        grid=(depth, bsz),
        in_specs=[pl.BlockSpec((1, n_mem, D_MODEL), lambda l, b: (b, 0, 0)),
                  pl.BlockSpec((1, 1, D_MODEL), lambda l, b: (l, 0, 0)), w_spec, w_spec],
        out_specs=[o_spec, o_spec],
        out_shape=[shape, shape],
        compiler_params=pltpu.CompilerParams(
            dimension_semantics=("arbitrary", "arbitrary"), vmem_limit_bytes=VMEM_LIMIT_BYTES),
        name="memory_kv",
    )(mem, norm_mem.reshape(depth, 1, D_MODEL), wk, wv)


def _pad_rows8(s):
    return jnp.pad(s, ((0, 0), (SUBLANES - s.shape[1], 0), (0, 0)))


def _block_diag(w):
    n, bw, _ = w.shape
    out = jnp.zeros((n * bw, n * bw), w.dtype)
    for i in range(n):
        out = out.at[i * bw:(i + 1) * bw, i * bw:(i + 1) * bw].set(w[i])
    return out


def _lane_pad(v, offset):
    return jnp.zeros((LANES,), F32).at[offset:offset + v.shape[0]].set(v)


def kernel(x_prompt, x_sample, cache_mem_k, cache_mem_v, state_a_conv, state_a_S, state_b_conv, state_b_h, state_c_S, state_ffn_conv, mem_prompt, norm_mix, w_in, a_conv_w, a_A_log, a_dt_bias, a_norm, b_conv_w, b_conv_b, b_w_r, b_b_r, b_w_i, b_b_i, b_lambda, c_lb_logits, c_norm, w_out, norm_attn, norm_mem, w_q, w_k, w_v, w_o, norm_ffn, w_up, ffn_conv_w, ffn_conv_b, w_down, norm_final):
    depth = w_in.shape[0]
    bsz = x_prompt.shape[0]
    dbsz = x_sample.shape[0]

    def row(v):
        return v.reshape(1, -1)

    mix_w, ffn_w = [], []
    for l in range(depth):
        w = w_in[l]
        pad = jnp.zeros((D_MODEL, LANES - 2 * H_A), F32)
        w_pad = jnp.concatenate([w[:, :OFF_BA], w[:, OFF_BA:OFF_BA + 2 * H_A], pad, w[:, OFF_BA + 2 * H_A:]],
                                axis=1).astype(BF16)
        a_head = jnp.zeros((SUBLANES, LANES), F32)
        a_head = a_head.at[0].set(_lane_pad(a_A_log[l], H_A)).at[1].set(_lane_pad(a_dt_bias[l], H_A))
        mix_w.append([
            row(norm_mix[l]), w_pad, a_conv_w[l], a_head, row(a_norm[l]),
            b_conv_w[l], row(b_conv_b[l]), _block_diag(b_w_r[l]).astype(BF16), row(b_b_r[l]),
            _block_diag(b_w_i[l]).astype(BF16), row(b_b_i[l]), row(b_lambda[l]),
            c_lb_logits, row(jnp.tile(c_norm[l], H_C)), w_out[l].astype(BF16)])
        ffn_w.append([
            row(norm_attn[l]), w_q[l].astype(BF16), w_o[l].astype(BF16), row(norm_ffn[l]),
            w_up[l].astype(BF16), ffn_conv_w[l], row(ffn_conv_b[l]), w_down[l].astype(BF16),
            row(norm_final)])

    p_mem_k, p_mem_v = _memkv_call(mem_prompt, norm_mem, w_k.astype(BF16), w_v.astype(BF16))

    def run_group(x, mem_k, mem_v, a_conv, a_s, b_conv, b_h, c_s, f_conv):
        outs = []
        for l in range(depth):
            states = [_pad_rows8(a_conv[l]), a_s[l], _pad_rows8(b_conv[l]), b_h[l][:, None, :], c_s[l]]
            x, na_conv, na_s, nb_conv, nb_h, nc_s = _mix_call(l, x, states, mix_w[l])
            x, nf_conv = _ffn_call(l == depth - 1, x, mem_k[l], mem_v[l], _pad_rows8(f_conv[l]), ffn_w[l])
            outs.append((na_conv[:, SUBLANES - (A_CONV - 1):], na_s, nb_conv[:, SUBLANES - (B_CONV - 1):],
                         nb_h[:, 0], nc_s, nf_conv[:, SUBLANES - (FFN_CONV - 1):]))
        return x, [jnp.stack(t, axis=0) for t in zip(*outs)]

    def zeros(n, *shape):
        return jnp.zeros((depth, n) + shape, F32)

    y_prompt, p_states = run_group(
        x_prompt, p_mem_k, p_mem_v,
        zeros(bsz, A_CONV - 1, 3 * D_A), zeros(bsz, H_A, DK_A, DV_A), zeros(bsz, B_CONV - 1, D_B),
        zeros(bsz, D_B), zeros(bsz, H_C, DK_C, DK_C), zeros(bsz, FFN_CONV - 1, 2 * D_FF))
    y_sample, s_states = run_group(
        x_sample, cache_mem_k.reshape(depth, dbsz, N_MEM, D_MODEL), cache_mem_v.reshape(depth, dbsz, N_MEM, D_MODEL),
        state_a_conv, state_a_S, state_b_conv, state_b_h, state_c_S, state_ffn_conv)

    kv_shape = (depth, bsz, N_MEM, MEM_HEADS, MEM_HEAD_DIM)
    return (y_prompt, y_sample, p_mem_k.reshape(kv_shape), p_mem_v.reshape(kv_shape),
            *p_states, *s_states)
```

```python
import functools
import math

import jax
import jax.numpy as jnp
from jax import lax
from jax.experimental import pallas as pl
from jax.experimental.pallas import tpu as pltpu

F32 = jnp.float32
BF16 = jnp.bfloat16

D_MODEL = 1024
CHUNK = 64
D_A = 512
DK_A = 128
DV_A = 128
H_A = 4
A_CONV = 4
D_B = 256
B_BLOCKS = 4
B_CONV = 4
RG_C = 8.0
D_C = 256
H_C = 4
DK_C = 64
N_MEM = 256
MEM_HEADS = 4
MEM_HEAD_DIM = 256
D_FF = 2816
FFN_CONV = 3
RMS_EPS = 1e-6

LANES = 128
SUBLANES = 8
VMEM_LIMIT_BYTES = 56 * 1024 * 1024

OFF_QKV = 0
OFF_Z = 1536
OFF_BA = 2048
OFF_BX = 2176
OFF_BG = 2432
OFF_CQ = 2688
OFF_CF = 2944
OFF_CI = 3200
OFF_CG = 3456
N_PROJ = 3712
OFF_CK = 3712
OFF_G = 3968
PROJ_W = 4096
FFN_BLK = 256
N_FFN_BLK = D_FF // FFN_BLK

N_INST = 2
INST_SKEW = 12

_NT = (((1,), (1,)), ((), ()))
_TN = (((0,), (0,)), ((), ()))


def _mm(a, b):
    return jnp.dot(a.astype(BF16), b.astype(BF16), preferred_element_type=F32)


def _mm_nt(a, b):
    return lax.dot_general(a.astype(BF16), b.astype(BF16), _NT, preferred_element_type=F32)


def _split3(x):
    hi = x.astype(BF16)
    r = x - hi.astype(F32)
    mid = r.astype(BF16)
    lo = (r - mid.astype(F32)).astype(BF16)
    return hi, mid, lo


def _mm_sel(x, sel, parts=3):
    return sum(jnp.dot(part, sel, preferred_element_type=F32) for part in _split3(x)[:parts])


def _cumsum_rows(x):
    c = x.shape[0]
    tri3 = ((_iota((c, 3 * c), 1) % c) <= _iota((c, 3 * c), 0)).astype(BF16)
    return jnp.dot(tri3, jnp.concatenate(_split3(x), axis=0), preferred_element_type=F32)


def _rms(x, g):
    ms = jnp.mean(x * x, axis=-1, keepdims=True)
    return x * lax.rsqrt(ms + RMS_EPS) * g


def _silu(x):
    return x * jax.nn.sigmoid(x)


def _iota(shape, axis):
    return lax.broadcasted_iota(jnp.int32, shape, axis)


def _shift_rows(x, carry8, s):
    rolled = pltpu.roll(x, s, 0)
    head = jnp.where(_iota(carry8.shape, 0) < s, pltpu.roll(carry8, s, 0), rolled[0:SUBLANES])
    if x.shape[0] == SUBLANES:
        return head
    return jnp.concatenate([head, rolled[SUBLANES:]], axis=0)


def _causal_conv(x, carry8, w):
    width = w.shape[0]
    y = x * w[width - 1:width, :]
    for s in range(1, width):
        y = y + _shift_rows(x, carry8, s) * w[width - 1 - s:width - s, :]
    return y


def _lin_scan(a, u, h0):
    n = a.shape[0]
    row8 = _iota(a.shape, 0) % SUBLANES
    s = 1
    while s < SUBLANES:
        keep = row8 >= s
        a_s = jnp.where(keep, pltpu.roll(a, s, 0), 1.0)
        u_s = jnp.where(keep, pltpu.roll(u, s, 0), 0.0)
        u = a * u_s + u
        a = a * a_s
        s *= 2
    groups = []
    carry = h0
    for r in range(0, n, SUBLANES):
        groups.append(a[r:r + SUBLANES] * carry + u[r:r + SUBLANES])
        carry = groups[-1][SUBLANES - 1:SUBLANES]
    return jnp.concatenate(groups, axis=0)


def _tile_rows_masked(x, mask):
    return jnp.concatenate([x] * (mask.shape[0] // x.shape[0]), axis=0) * mask


def _chunk_masks():
    c = CHUNK
    n = H_A * c
    rowi = _iota((c, n), 0)
    lane_j = _iota((c, n), 1) % c
    m = {
        "half": _iota((c, LANES), 1) < c,
        "incl": lane_j <= rowi,
        "strict": lane_j < rowi,
        "eye": jnp.where(lane_j == rowi, 1.0, 0.0),
        "bd_sq": jnp.where(_iota((n, n), 0) // c == _iota((n, n), 1) // c, 1.0, 0.0).astype(BF16),
        "bd_wide": jnp.where(_iota((n, H_A * DV_A), 0) // c == _iota((n, H_A * DV_A), 1) // DV_A,
                             1.0, 0.0).astype(BF16),
        "ones3": jnp.ones((c, 3 * c), BF16),
    }
    for s in (32, 16, 8):
        m["odd", s] = ((_iota((c, D_C), 0) // s) % 2) == 1
        m["pair", s] = (rowi // (2 * s)) == (lane_j // (2 * s))
    for o in range(SUBLANES):
        m["valid", o] = (_iota((c, D_C), 0) % SUBLANES) >= o
        m["diag", o] = lane_j == rowi - o
    return m


def _each(fn, *lists):
    return [fn(*args) for args in zip(*lists)]


def _interleave(*gens):
    results = [None] * len(gens)
    live = dict(enumerate(gens))
    while live:
        for i, g in list(live.items()):
            try:
                next(g)
            except StopIteration as done:
                results[i] = done.value
                del live[i]
        yield
    return results


def _interleave_skewed(gens, skew):
    live = list(gens)
    step = 0
    while live:
        for k, g in enumerate(gens):
            if g in live and step >= k * skew:
                try:
                    next(g)
                except StopIteration:
                    live.remove(g)
        step += 1


def _gdn_prep(qs, ks, vs, beta_blks, g_blks, m):
    c = CHUNK
    dot = functools.partial(jnp.dot, preferred_element_type=F32)
    gcum = _each(_cumsum_rows, g_blks)
    yield

    def lane_forms(blk, first):
        cols = [jnp.broadcast_to(blk[:, first + h:first + h + 1], (c, LANES)) for h in range(H_A)]
        wide = jnp.concatenate(cols, axis=1)
        sq = jnp.concatenate([jnp.where(m["half"], cols[0], cols[1]), jnp.where(m["half"], cols[2], cols[3])],
                             axis=1)
        return wide, sq

    g_forms = _each(lambda g: lane_forms(g, H_A), gcum)
    yield
    b_forms = _each(lambda b: lane_forms(b, 0), beta_blks)
    yield
    gr = _each(lambda gf: dot(m["ones3"], jnp.concatenate(_split3(gf[1] * m["eye"]), axis=0)), g_forms)
    yield
    gamma = _each(lambda gf, r: jnp.exp(jnp.where(m["incl"], gf[1] - r, -jnp.inf)), g_forms, gr)
    yield
    kb = _each(lambda k: k.astype(BF16), ks)
    yield
    qk_kk = _each(lambda q, k_b: lax.dot_general(jnp.concatenate([q.astype(BF16), k_b], axis=0),
                                                 _tile_rows_masked(k_b, m["bd_wide"]), _NT,
                                                 preferred_element_type=F32), qs, kb)
    yield
    qk = _each(lambda r, gm: (r[:c] * gm).astype(BF16), qk_kk, gamma)
    yield
    a = _each(lambda bf, r, gm: jnp.where(m["strict"], bf[1] * r[c:] * gm, 0.0), b_forms, qk_kk, gamma)
    yield
    x = _each(lambda a_: m["eye"] - a_, a)
    yield
    ab = _each(lambda a_: a_.astype(BF16), a)
    yield
    p = _each(lambda a_b: dot(a_b, _tile_rows_masked(a_b, m["bd_sq"])), ab)
    yield
    for it in range(5):
        pb = _each(lambda p_: p_.astype(BF16), p)
        yield
        p_bd = _each(lambda p_b: _tile_rows_masked(p_b, m["bd_sq"]), pb)
        yield
        if it < 4:
            both = _each(lambda x_, p_b, bd: dot(jnp.concatenate([x_.astype(BF16), p_b], axis=0), bd), x, pb, p_bd)
            x = _each(lambda x_, r: x_ + r[:c], x, both)
            p = _each(lambda r: r[c:], both)
        else:
            x = _each(lambda x_, bd: x_ + dot(x_.astype(BF16), bd), x, p_bd)
    xb = _each(lambda x_: x_.astype(BF16), x)
    yield
    eg = _each(lambda gf: jnp.exp(gf[0]), g_forms)
    yield
    u = _each(lambda x_b, v, bf: dot(x_b, _tile_rows_masked((v * bf[0]).astype(BF16), m["bd_wide"])),
              xb, vs, b_forms)
    yield
    w = _each(lambda x_b, k, bf, e: dot(x_b, _tile_rows_masked((k * (bf[0] * e)).astype(BF16), m["bd_wide"])),
              xb, ks, b_forms, eg)
    yield
    glast = _each(lambda g: jnp.concatenate(
        [jnp.broadcast_to(g[c - 1:c, H_A + h:H_A + h + 1], (1, DV_A)) for h in range(H_A)], axis=1), gcum)
    yield
    kd = _each(lambda k, gl, gf: (k * jnp.exp(gl - gf[0])).astype(BF16), ks, glast, g_forms)
    yield
    kd_stack = _each(lambda kd_: jnp.concatenate([kd_[:, h * DK_A:(h + 1) * DK_A] for h in range(H_A)], axis=0), kd)
    yield
    wq = _each(lambda w_, q, e: [
        jnp.concatenate([w_[:, h * DK_A:(h + 1) * DK_A], (q * e)[:, h * DK_A:(h + 1) * DK_A]], axis=0).astype(BF16)
        for h in range(H_A)], w, qs, eg)
    decay = _each(jnp.exp, glast)
    yield
    return list(zip(u, wq, qk, kd_stack, decay))


def _gdn_step(prep, s_ref, m):
    u, wq, qk, kd_stack, decay = prep
    c = CHUNK
    s = s_ref[...]
    sb = s.astype(BF16)
    v_new, o_state = [], []
    for h in range(H_A):
        hs = slice(h * DV_A, (h + 1) * DV_A)
        ws = jnp.dot(wq[h], sb[:, hs], preferred_element_type=F32)
        v_new.append(u[:, hs] - ws[:c])
        o_state.append(ws[c:])
    v_bd = _tile_rows_masked(jnp.concatenate(v_new, axis=1).astype(BF16), m["bd_wide"])
    o = jnp.concatenate(o_state, axis=1) + jnp.dot(qk, v_bd, preferred_element_type=F32)
    s_ref[...] = s * decay + lax.dot_general(kd_stack, v_bd, _TN, preferred_element_type=F32)
    return o


def _hgrn_prep(qs, ks, vs, lfs, m):
    c = CHUNK
    d = D_C
    dot = functools.partial(jnp.dot, preferred_element_type=F32)
    bc = _each(_cumsum_rows, lfs)
    yield
    att = [jnp.zeros((c, H_C * c), F32) for _ in qs]
    for s in (32, 16, 8):
        odd = m["odd", s]
        ref = _each(lambda b: jnp.concatenate(
            [jnp.broadcast_to(b[r:r + 1, :], (2 * s, d)) for r in range(s, c, 2 * s)], axis=0), bc)
        yield
        e = _each(lambda b, rf: jnp.exp(jnp.where(odd, b - rf, rf - b)), bc, ref)
        yield
        qt = _each(lambda q, e_: jnp.where(odd, q * e_, 0.0).astype(BF16), qs, e)
        yield
        kt = _each(lambda k, e_: jnp.where(odd, 0.0, k * e_).astype(BF16), ks, e)
        yield
        lvl = _each(lambda q_t, k_t: lax.dot_general(q_t, _tile_rows_masked(k_t, m["bd_sq"]), _NT,
                                                     preferred_element_type=F32), qt, kt)
        yield
        att = _each(lambda a_, l: a_ + jnp.where(m["pair", s], l, 0.0), att, lvl)
        yield

    def shifted_prods(q, k, b):
        out = []
        for o in range(SUBLANES):
            ko = k if o == 0 else pltpu.roll(k, o, 0)
            bo = b if o == 0 else pltpu.roll(b, o, 0)
            out.append((q * ko * jnp.exp(jnp.where(m["valid", o], b - bo, -jnp.inf))).astype(BF16))
        return jnp.concatenate(out, axis=0)

    prods = _each(shifted_prods, qs, ks, bc)
    yield
    dsum = _each(lambda pr: dot(pr, m["bd_sq"]), prods)
    yield
    for o in range(SUBLANES):
        att = _each(lambda a_, ds: a_ + jnp.where(m["diag", o], ds[o * c:(o + 1) * c], 0.0), att, dsum)
        yield
    vb = _each(lambda v: v.astype(BF16), vs)
    yield
    o_intra = _each(lambda a_, v_b: dot(a_.astype(BF16), _tile_rows_masked(v_b, m["bd_sq"])), att, vb)
    yield
    qdec = _each(lambda q, b: (q * jnp.exp(b)).astype(BF16), qs, bc)
    yield
    kdec = _each(lambda k, b: (k * jnp.exp(b[c - 1:c, :] - b)).astype(BF16), ks, bc)
    yield
    decay = _each(lambda b: jnp.exp(b[c - 1:c, :]), bc)
    yield
    return list(zip(o_intra, qdec, kdec, vb, decay))


def _hgrn_step(prep, st_ref, m):
    o_intra, qdec, kdec, vb, decay = prep
    st = st_ref[...]
    o = o_intra + lax.dot_general(qdec, st.astype(BF16), _NT, preferred_element_type=F32)
    upd = lax.dot_general(vb, kdec, _TN, preferred_element_type=F32)
    st_ref[...] = st * decay + upd * m["bd_sq"].astype(F32)
    return o


def _mix_tile(layer, tile, x_ref, y_ref, proj, cat, aconv_s, sa_s, bconv_s, bh_s, sc_s, w, masks):
    (gmix_ref, win_ref, aconvw_ref, ahead_ref, anorm_ref, bconvw_ref, bconvb_ref, wr_ref, br_ref,
     wi_ref, bi_ref, blam_ref, clb_ref, cnorm_ref, wout_ref) = w
    d = D_C
    x = x_ref[...]
    h_in = _rms(x, gmix_ref[...]).astype(BF16)
    yield

    def in_proj(lo, width):
        return jnp.dot(h_in, win_ref[:, lo:lo + width], preferred_element_type=F32)

    for j in range(3 * D_A // FFN_BLK):
        lo = j * FFN_BLK
        pre = in_proj(lo, FFN_BLK)
        yield
        act = _silu(_causal_conv(pre, aconv_s[:, lo:lo + FFN_BLK], aconvw_ref[:, lo:lo + FFN_BLK]))
        aconv_s[:, lo:lo + FFN_BLK] = pre[tile - SUBLANES:tile]
        if lo < 2 * D_A:
            for hh in range(FFN_BLK // DK_A):
                blk = act[:, hh * DK_A:(hh + 1) * DK_A]
                nrm = blk * lax.rsqrt(jnp.sum(blk * blk, axis=-1, keepdims=True) + 1e-6)
                if lo < D_A:
                    nrm = nrm * (DK_A ** -0.5)
                proj[:, lo + hh * DK_A:lo + (hh + 1) * DK_A] = nrm
        else:
            proj[:, lo:lo + FFN_BLK] = act
        yield
    ba = in_proj(OFF_BA, LANES)
    proj[:, OFF_G:OFF_G + LANES] = (-jnp.exp(ahead_ref[0:1, :])) * jax.nn.softplus(ba + ahead_ref[1:2, :])
    proj[:, OFF_BA:OFF_BA + LANES] = jax.nn.sigmoid(ba)
    yield

    bpre = in_proj(OFF_BX, D_B)
    yield
    xb = _causal_conv(bpre, bconv_s[...], bconvw_ref[...]) + bconvb_ref[...]
    bconv_s[...] = bpre[tile - SUBLANES:tile]
    r_gate = jax.nn.sigmoid(_mm(xb, wr_ref[...]) + br_ref[...])
    i_gate = jax.nn.sigmoid(_mm(xb, wi_ref[...]) + bi_ref[...])
    yield
    log_a = (-RG_C) * r_gate * jax.nn.softplus(-blam_ref[...])
    a_gate = jnp.exp(log_a)
    one_m_a2 = -jnp.tanh(log_a) * (a_gate * a_gate + 1.0)
    mult = jnp.where(one_m_a2 > 0.0, one_m_a2 * lax.rsqrt(one_m_a2), 0.0)
    yield
    for lo in (OFF_Z, OFF_Z + FFN_BLK, OFF_CI, OFF_CG):
        proj[:, lo:lo + FFN_BLK] = in_proj(lo, FFN_BLK)
        yield
    hb = _lin_scan(a_gate, mult * i_gate * xb, bh_s[...])
    bh_s[...] = hb[tile - 1:tile]
    yield
    cat[:, D_A:D_A + D_B] = hb * jax.nn.gelu(in_proj(OFF_BG, D_B))
    yield

    lg = clb_ref[...]
    ex = jnp.exp(lg - jnp.max(lg, axis=0, keepdims=True))
    sm = ex / jnp.sum(ex, axis=0, keepdims=True)
    lb = jnp.sum(sm[0:layer + 1], axis=0, keepdims=True) - sm[0:1]
    forget = lb + (1.0 - lb) * jax.nn.sigmoid(in_proj(OFF_CF, d))
    proj[:, OFF_CK:OFF_CK + d] = 1.0 - forget
    proj[:, OFF_CF:OFF_CF + d] = jnp.log(forget)
    yield
    proj[:, OFF_CQ:OFF_CQ + d] = _silu(in_proj(OFF_CQ, d))
    yield

    chunks = [slice(ci * CHUNK, (ci + 1) * CHUNK) for ci in range(tile // CHUNK)]

    def cols(lo, width):
        return [proj[rows, lo:lo + width] for rows in chunks]

    gdn = _gdn_prep(cols(0, D_A), cols(D_A, D_A), cols(2 * D_A, D_A), cols(OFF_BA, LANES), cols(OFF_G, LANES), masks)
    hgrn = _hgrn_prep(cols(OFF_CQ, d), cols(OFF_CK, d), cols(OFF_CI, d), cols(OFF_CF, d), masks)
    gdn_preps, hgrn_preps = yield from _interleave(gdn, hgrn)
    for rows, gdn_prep, hgrn_prep in zip(chunks, gdn_preps, hgrn_preps):
        cat[rows, 0:D_A] = _gdn_step(gdn_prep, sa_s, masks)
        yield
        cat[rows, D_A + D_B:D_MODEL] = _hgrn_step(hgrn_prep, sc_s, masks)
        yield

    for h in range(H_A):
        blk = cat[:, h * DV_A:(h + 1) * DV_A]
        ms = jnp.mean(blk * blk, axis=-1, keepdims=True)
        z = proj[:, OFF_Z + h * DV_A:OFF_Z + (h + 1) * DV_A]
        cat[:, h * DV_A:(h + 1) * DV_A] = blk * lax.rsqrt(ms + RMS_EPS) * anorm_ref[...] * _silu(z)
        yield
    oc = cat[:, D_A + D_B:D_MODEL]
    ms_c = _mm_sel(oc * oc, masks["bd_sq"], parts=2) * (1.0 / DK_C)
    cat[:, D_A + D_B:D_MODEL] = (oc * lax.rsqrt(ms_c + RMS_EPS) * cnorm_ref[...]
                                 * _silu(proj[:, OFF_CG:OFF_CG + d]))
    yield
    y_ref[...] = x + _mm(cat[...], wout_ref[...])


def _mix_kernel(layer, x_ref, aconv_ref, as_ref, bconv_ref, bh_ref, cs_ref, *rest):
    w = rest[:15]
    y_ref, aconv_o, as_o, bconv_o, bh_o, cs_o, proj, cat, aconv_s, sa_s, bconv_s, bh_s, sc_s = rest[15:]
    t = pl.program_id(1)
    nt = pl.num_programs(1)
    n_inst, tile = x_ref.shape[0], x_ref.shape[1]
    d = D_C
    masks = _chunk_masks()

    @pl.when(t == 0)
    def _load_state():
        for i in range(n_inst):
            aconv_s[i] = aconv_ref[i]
            bconv_s[i] = bconv_ref[i]
            bh_s[i] = bh_ref[i]
            for h in range(H_A):
                sa_s[i, :, h * DV_A:(h + 1) * DV_A] = as_ref[i, h]
            rows = jnp.concatenate([cs_ref[i, h] for h in range(H_C)], axis=0)
            tile_sel = jnp.where(_iota((DK_C, d), 1) % DK_C == _iota((DK_C, d), 0), 1.0, 0.0).astype(BF16)
            s_bd = _mm_sel(rows, tile_sel) * masks["bd_sq"].astype(F32)
            sc_s[i] = s_bd.T

    _interleave_skewed(
        [_mix_tile(layer, tile, x_ref.at[i], y_ref.at[i], proj.at[i], cat.at[i], aconv_s.at[i], sa_s.at[i],
                   bconv_s.at[i], bh_s.at[i], sc_s.at[i], w, masks) for i in range(n_inst)], INST_SKEW)

    @pl.when(t == nt - 1)
    def _store_state():
        for i in range(n_inst):
            aconv_o[i] = aconv_s[i]
            bconv_o[i] = bconv_s[i]
            bh_o[i] = bh_s[i]
            for h in range(H_A):
                as_o[i, h] = sa_s[i, :, h * DV_A:(h + 1) * DV_A]
            s_bd = sc_s[i].T
            for h in range(H_C):
                sel = jnp.where(_iota((d, DK_C), 0) == _iota((d, DK_C), 1) + h * DK_C, 1.0, 0.0).astype(BF16)
                cs_o[i, h] = _mm_sel(s_bd[h * DK_C:(h + 1) * DK_C, :], sel)


def _ffn_kernel(final, x_ref, mk_ref, mv_ref, fconv_ref, gattn_ref, wq_ref, wo_ref,
                gffn_ref, wup_ref, fconvw_ref, fconvb_ref, wdown_ref, gfin_ref,
                y_ref, fconv_o, kb_s, vb_s, carry_s, act_s):
    t = pl.program_id(1)
    nt = pl.num_programs(1)
    tile = x_ref.shape[1]

    @pl.when(t == 0)
    def _load():
        kb_s[...] = mk_ref[0].astype(BF16)
        vb_s[...] = mv_ref[0].astype(BF16)
        carry_s[...] = fconv_ref[0]

    x = x_ref[0]
    hq = _rms(x, gattn_ref[...]).astype(BF16)
    q = jnp.dot(hq, wq_ref[...], preferred_element_type=F32)
    head_cols = [slice(h * MEM_HEAD_DIM, (h + 1) * MEM_HEAD_DIM) for h in range(MEM_HEADS)]
    scores = [_mm_nt(q[:, sl], kb_s[:, sl]) * (MEM_HEAD_DIM ** -0.5) for sl in head_cols]
    expd = [jnp.exp(s - jnp.max(s, axis=-1, keepdims=True)) for s in scores]
    probs = [e / jnp.sum(e, axis=-1, keepdims=True) for e in expd]
    heads = [_mm(p, vb_s[:, sl]) for p, sl in zip(probs, head_cols)]
    x1 = x + _mm(jnp.concatenate(heads, axis=1), wo_ref[...])

    hf = _rms(x1, gffn_ref[...]).astype(BF16)
    for j in range(N_FFN_BLK):
        halves = []
        for off in (j * FFN_BLK, D_FF + j * FFN_BLK):
            u = jnp.dot(hf, wup_ref[:, off:off + FFN_BLK], preferred_element_type=F32)
            halves.append(_causal_conv(u, carry_s[:, off:off + FFN_BLK], fconvw_ref[:, off:off + FFN_BLK])
                          + fconvb_ref[:, off:off + FFN_BLK])
            carry_s[:, off:off + FFN_BLK] = u[tile - SUBLANES:tile]
        act_s[:, j * FFN_BLK:(j + 1) * FFN_BLK] = (_silu(halves[0]) * halves[1]).astype(BF16)
    y = x1 + jnp.dot(act_s[...], wdown_ref[...], preferred_element_type=F32)
    if final:
        y = _rms(y, gfin_ref[...])
    y_ref[0] = y

    @pl.when(t == nt - 1)
    def _store():
        fconv_o[0] = carry_s[...]


def _memkv_kernel(mem_ref, g_ref, wk_ref, wv_ref, k_ref, v_ref):
    mn = _rms(mem_ref[0], g_ref[0]).astype(BF16)
    k_ref[0, 0] = jnp.dot(mn, wk_ref[0], preferred_element_type=F32)
    v_ref[0, 0] = jnp.dot(mn, wv_ref[0], preferred_element_type=F32)


def _resident(shape):
    nd = len(shape)
    return pl.BlockSpec(shape, lambda b, t: (0,) * nd, pipeline_mode=pl.Buffered(1))


def _per_batch(shape):
    nd = len(shape)
    return pl.BlockSpec((1,) + tuple(shape[1:]), lambda b, t: (b,) + (0,) * (nd - 1))


def _row_group(shape):
    nd = len(shape)
    return pl.BlockSpec((N_INST,) + tuple(shape[1:]), lambda g, t: (g,) + (0,) * (nd - 1))


def _seq_tile(length):
    return 256 if length % 256 == 0 else CHUNK


def _mix_call(layer, x, states, wts):
    bsz, length, _ = x.shape
    tile = _seq_tile(length)
    x_spec = pl.BlockSpec((N_INST, tile, D_MODEL), lambda g, t: (g, t, 0))
    state_specs = [_row_group(s.shape) for s in states]
    out_shapes = [jax.ShapeDtypeStruct(x.shape, F32)] + [jax.ShapeDtypeStruct(s.shape, F32) for s in states]
    return pl.pallas_call(
        functools.partial(_mix_kernel, layer),
        grid=(bsz // N_INST, length // tile),
        in_specs=[x_spec] + state_specs + [_resident(w.shape) for w in wts],
        out_specs=[x_spec] + state_specs,
        out_shape=out_shapes,
        scratch_shapes=[
            pltpu.VMEM((N_INST, tile, PROJ_W), F32),
            pltpu.VMEM((N_INST, tile, D_MODEL), F32),
            pltpu.VMEM((N_INST, SUBLANES, 3 * D_A), F32),
            pltpu.VMEM((N_INST, DK_A, H_A * DV_A), F32),
            pltpu.VMEM((N_INST, SUBLANES, D_B), F32),
            pltpu.VMEM((N_INST, 1, D_B), F32),
            pltpu.VMEM((N_INST, D_C, D_C), F32),
        ],
        compiler_params=pltpu.CompilerParams(
            dimension_semantics=("arbitrary", "arbitrary"), vmem_limit_bytes=VMEM_LIMIT_BYTES),
        name="mix_block",
    )(x, *states, *wts)


def _ffn_call(final, x, mem_k, mem_v, fconv, wts):
    bsz, length, _ = x.shape
    tile = _seq_tile(length)
    x_spec = pl.BlockSpec((1, tile, D_MODEL), lambda b, t: (b, t, 0))
    return pl.pallas_call(
        functools.partial(_ffn_kernel, final),
        grid=(bsz, length // tile),
        in_specs=[x_spec, _per_batch(mem_k.shape), _per_batch(mem_v.shape), _per_batch(fconv.shape)]
        + [_resident(w.shape) for w in wts],
        out_specs=[x_spec, _per_batch(fconv.shape)],
        out_shape=[jax.ShapeDtypeStruct(x.shape, F32), jax.ShapeDtypeStruct(fconv.shape, F32)],
        scratch_shapes=[
            pltpu.VMEM((N_MEM, D_MODEL), BF16),
            pltpu.VMEM((N_MEM, D_MODEL), BF16),
            pltpu.VMEM((SUBLANES, 2 * D_FF), F32),
            pltpu.VMEM((tile, D_FF), BF16),
        ],
        compiler_params=pltpu.CompilerParams(
            dimension_semantics=("arbitrary", "arbitrary"), vmem_limit_bytes=VMEM_LIMIT_BYTES),
        name="attn_ffn_block",
    )(x, mem_k, mem_v, fconv, *wts)


def _memkv_call(mem, norm_mem, wk, wv):
    depth = wk.shape[0]
    bsz, n_mem, _ = mem.shape
    w_spec = pl.BlockSpec((1, D_MODEL, D_MODEL), lambda l, b: (l, 0, 0))
    o_spec = pl.BlockSpec((1, 1, n_mem, D_MODEL), lambda l, b: (l, b, 0, 0))
    shape = jax.ShapeDtypeStruct((depth, bsz, n_mem, D_MODEL), F32)
    return pl.pallas_call(
        _memkv_kernel,
        grid=(depth, bsz),
        in_specs=[pl.BlockSpec((1, n_mem, D_MODEL), lambda l, b: (b, 0, 0)),
                  pl.BlockSpec((1, 1, D_MODEL), lambda l, b: (l, 0, 0)), w_spec, w_spec],
        out_specs=[o_spec, o_spec],
        out_shape=[shape, shape],
        compiler_params=pltpu.CompilerParams(
            dimension_semantics=("arbitrary", "arbitrary"), vmem_limit_bytes=VMEM_LIMIT_BYTES),
        name="memory_kv",
    )(mem, norm_mem.reshape(depth, 1, D_MODEL), wk, wv)


def _pad_rows8(s):
    return jnp.pad(s, ((0, 0), (SUBLANES - s.shape[1], 0), (0, 0)))


def _block_diag(w):
    n, bw, _ = w.shape
    out = jnp.zeros((n * bw, n * bw), w.dtype)
    for i in range(n):
        out = out.at[i * bw:(i + 1) * bw, i * bw:(i + 1) * bw].set(w[i])
    return out


def _lane_pad(v, offset):
    return jnp.zeros((LANES,), F32).at[offset:offset + v.shape[0]].set(v)


def kernel(x_prompt, x_sample, cache_mem_k, cache_mem_v, state_a_conv, state_a_S, state_b_conv, state_b_h, state_c_S, state_ffn_conv, mem_prompt, norm_mix, w_in, a_conv_w, a_A_log, a_dt_bias, a_norm, b_conv_w, b_conv_b, b_w_r, b_b_r, b_w_i, b_b_i, b_lambda, c_lb_logits, c_norm, w_out, norm_attn, norm_mem, w_q, w_k, w_v, w_o, norm_ffn, w_up, ffn_conv_w, ffn_conv_b, w_down, norm_final):
    depth = w_in.shape[0]
    bsz = x_prompt.shape[0]
    dbsz = x_sample.shape[0]
    assert bsz % N_INST == 0 and dbsz % N_INST == 0

    def row(v):
        return v.reshape(1, -1)

    mix_w, ffn_w = [], []
    for l in range(depth):
        w = w_in[l]
        pad = jnp.zeros((D_MODEL, LANES - 2 * H_A), F32)
        w_pad = jnp.concatenate([w[:, :OFF_BA], w[:, OFF_BA:OFF_BA + 2 * H_A], pad, w[:, OFF_BA + 2 * H_A:]],
                                axis=1).astype(BF16)
        a_head = jnp.zeros((SUBLANES, LANES), F32)
        a_head = a_head.at[0].set(_lane_pad(a_A_log[l], H_A)).at[1].set(_lane_pad(a_dt_bias[l], H_A))
        mix_w.append([
            row(norm_mix[l]), w_pad, a_conv_w[l], a_head, row(a_norm[l]),
            b_conv_w[l], row(b_conv_b[l]), _block_diag(b_w_r[l]).astype(BF16), row(b_b_r[l]),
            _block_diag(b_w_i[l]).astype(BF16), row(b_b_i[l]), row(b_lambda[l]),
            c_lb_logits, row(jnp.tile(c_norm[l], H_C)), w_out[l].astype(BF16)])
        ffn_w.append([
            row(norm_attn[l]), w_q[l].astype(BF16), w_o[l].astype(BF16), row(norm_ffn[l]),
            w_up[l].astype(BF16), ffn_conv_w[l], row(ffn_conv_b[l]), w_down[l].astype(BF16),
            row(norm_final)])

    p_mem_k, p_mem_v = _memkv_call(mem_prompt, norm_mem, w_k.astype(BF16), w_v.astype(BF16))

    def run_group(x, mem_k, mem_v, a_conv, a_s, b_conv, b_h, c_s, f_conv):
        outs = []
        for l in range(depth):
            states = [_pad_rows8(a_conv[l]), a_s[l], _pad_rows8(b_conv[l]), b_h[l][:, None, :], c_s[l]]
            x, na_conv, na_s, nb_conv, nb_h, nc_s = _mix_call(l, x, states, mix_w[l])
            x, nf_conv = _ffn_call(l == depth - 1, x, mem_k[l], mem_v[l], _pad_rows8(f_conv[l]), ffn_w[l])
            outs.append((na_conv[:, SUBLANES - (A_CONV - 1):], na_s, nb_conv[:, SUBLANES - (B_CONV - 1):],
                         nb_h[:, 0], nc_s, nf_conv[:, SUBLANES - (FFN_CONV - 1):]))
        return x, [jnp.stack(t, axis=0) for t in zip(*outs)]

    def zeros(n, *shape):
        return jnp.zeros((depth, n) + shape, F32)

    y_prompt, p_states = run_group(
        x_prompt, p_mem_k, p_mem_v,
        zeros(bsz, A_CONV - 1, 3 * D_A), zeros(bsz, H_A, DK_A, DV_A), zeros(bsz, B_CONV - 1, D_B),
        zeros(bsz, D_B), zeros(bsz, H_C, DK_C, DK_C), zeros(bsz, FFN_CONV - 1, 2 * D_FF))
    y_sample, s_states = run_group(
        x_sample, cache_mem_k.reshape(depth, dbsz, N_MEM, D_MODEL), cache_mem_v.reshape(depth, dbsz, N_MEM, D_MODEL),
        state_a_conv, state_a_S, state_b_conv, state_b_h, state_c_S, state_ffn_conv)

    kv_shape = (depth, bsz, N_MEM, MEM_HEADS, MEM_HEAD_DIM)
    return (y_prompt, y_sample, p_mem_k.reshape(kv_shape), p_mem_v.reshape(kv_shape),
            *p_states, *s_states)
```

```python
import functools
import math

import jax
import jax.numpy as jnp
from jax import lax
from jax.experimental import pallas as pl
from jax.experimental.pallas import tpu as pltpu

F32 = jnp.float32
BF16 = jnp.bfloat16

D_MODEL = 1024
CHUNK = 64
D_A = 512
DK_A = 128
DV_A = 128
H_A = 4
A_CONV = 4
D_B = 256
B_BLOCKS = 4
B_CONV = 4
RG_C = 8.0
D_C = 256
H_C = 4
DK_C = 64
N_MEM = 256
MEM_HEADS = 4
MEM_HEAD_DIM = 256
D_FF = 2816
FFN_CONV = 3
RMS_EPS = 1e-6

LANES = 128
SUBLANES = 8
VMEM_LIMIT_BYTES = 56 * 1024 * 1024

OFF_QKV = 0
OFF_Z = 1536
OFF_BA = 2048
OFF_BX = 2176
OFF_BG = 2432
OFF_CQ = 2688
OFF_CF = 2944
OFF_CI = 3200
OFF_CG = 3456
N_PROJ = 3712
OFF_CK = 3712
OFF_G = 3968
PROJ_W = 4096
FFN_BLK = 256
N_FFN_BLK = D_FF // FFN_BLK

N_LAYERS = 2

MP_ACONV, MP_NORM, MP_ALOG, MP_DT, MP_ANORM = 0, 4, 5, 6, 7
MP_BCONV, MP_BCONVB, MP_BR, MP_BI, MP_LAMBDA, MP_CNORM, MP_CLB = 8, 12, 13, 14, 15, 16, 17
FP_CONV, FP_CONVB, FP_NORM_ATTN, FP_NORM_FFN, FP_NORM_FINAL = 0, 3, 4, 5, 6

N_INST = 2
INST_SKEW = 12

_NT = (((1,), (1,)), ((), ()))
_TN = (((0,), (0,)), ((), ()))


def _mm(a, b):
    return jnp.dot(a.astype(BF16), b.astype(BF16), preferred_element_type=F32)


def _mm_nt(a, b):
    return lax.dot_general(a.astype(BF16), b.astype(BF16), _NT, preferred_element_type=F32)


def _split3(x):
    hi = x.astype(BF16)
    r = x - hi.astype(F32)
    mid = r.astype(BF16)
    lo = (r - mid.astype(F32)).astype(BF16)
    return hi, mid, lo


def _mm_sel(x, sel, parts=3):
    return sum(jnp.dot(part, sel, preferred_element_type=F32) for part in _split3(x)[:parts])


def _cumsum_rows(x):
    c = x.shape[0]
    tri3 = ((_iota((c, 3 * c), 1) % c) <= _iota((c, 3 * c), 0)).astype(BF16)
    return jnp.dot(tri3, jnp.concatenate(_split3(x), axis=0), preferred_element_type=F32)


def _rms(x, g):
    ms = jnp.mean(x * x, axis=-1, keepdims=True)
    return x * lax.rsqrt(ms + RMS_EPS) * g


def _silu(x):
    return x * jax.nn.sigmoid(x)


def _iota(shape, axis):
    return lax.broadcasted_iota(jnp.int32, shape, axis)


def _shift_rows(x, carry8, s):
    rolled = pltpu.roll(x, s, 0)
    head = jnp.where(_iota(carry8.shape, 0) < s, pltpu.roll(carry8, s, 0), rolled[0:SUBLANES])
    if x.shape[0] == SUBLANES:
        return head
    return jnp.concatenate([head, rolled[SUBLANES:]], axis=0)


def _causal_conv(x, carry8, w):
    width = w.shape[0]
    y = x * w[width - 1:width, :]
    for s in range(1, width):
        y = y + _shift_rows(x, carry8, s) * w[width - 1 - s:width - s, :]
    return y


def _lin_scan(a, u, h0):
    n = a.shape[0]
    row8 = _iota(a.shape, 0) % SUBLANES
    s = 1
    while s < SUBLANES:
        keep = row8 >= s
        a_s = jnp.where(keep, pltpu.roll(a, s, 0), 1.0)
        u_s = jnp.where(keep, pltpu.roll(u, s, 0), 0.0)
        u = a * u_s + u
        a = a * a_s
        s *= 2
    groups = []
    carry = h0
    for r in range(0, n, SUBLANES):
        groups.append(a[r:r + SUBLANES] * carry + u[r:r + SUBLANES])
        carry = groups[-1][SUBLANES - 1:SUBLANES]
    return jnp.concatenate(groups, axis=0)


def _tile_rows_masked(x, mask):
    return jnp.concatenate([x] * (mask.shape[0] // x.shape[0]), axis=0) * mask


def _chunk_masks():
    c = CHUNK
    n = H_A * c
    rowi = _iota((c, n), 0)
    lane_j = _iota((c, n), 1) % c
    m = {
        "half": _iota((c, LANES), 1) < c,
        "incl": lane_j <= rowi,
        "strict": lane_j < rowi,
        "eye": jnp.where(lane_j == rowi, 1.0, 0.0),
        "bd_sq": jnp.where(_iota((n, n), 0) // c == _iota((n, n), 1) // c, 1.0, 0.0).astype(BF16),
        "bd_wide": jnp.where(_iota((n, H_A * DV_A), 0) // c == _iota((n, H_A * DV_A), 1) // DV_A,
                             1.0, 0.0).astype(BF16),
        "ones3": jnp.ones((c, 3 * c), BF16),
    }
    for s in (32, 16, 8):
        m["odd", s] = ((_iota((c, D_C), 0) // s) % 2) == 1
        m["pair", s] = (rowi // (2 * s)) == (lane_j // (2 * s))
    for o in range(SUBLANES):
        m["valid", o] = (_iota((c, D_C), 0) % SUBLANES) >= o
        m["diag", o] = lane_j == rowi - o
    return m


def _each(fn, *lists):
    return [fn(*args) for args in zip(*lists)]


def _interleave(*gens):
    results = [None] * len(gens)
    live = dict(enumerate(gens))
    while live:
        for i, g in list(live.items()):
            try:
                next(g)
            except StopIteration as done:
                results[i] = done.value
                del live[i]
        yield
    return results


def _interleave_skewed(gens, skew):
    live = list(gens)
    step = 0
    while live:
        for k, g in enumerate(gens):
            if g in live and step >= k * skew:
                try:
                    next(g)
                except StopIteration:
                    live.remove(g)
        step += 1


def _gdn_prep(qs, ks, vs, beta_blks, g_blks, m):
    c = CHUNK
    dot = functools.partial(jnp.dot, preferred_element_type=F32)
    gcum = _each(_cumsum_rows, g_blks)
    yield

    def lane_forms(blk, first):
        cols = [jnp.broadcast_to(blk[:, first + h:first + h + 1], (c, LANES)) for h in range(H_A)]
        wide = jnp.concatenate(cols, axis=1)
        sq = jnp.concatenate([jnp.where(m["half"], cols[0], cols[1]), jnp.where(m["half"], cols[2], cols[3])],
                             axis=1)
        return wide, sq

    g_forms = _each(lambda g: lane_forms(g, H_A), gcum)
    yield
    b_forms = _each(lambda b: lane_forms(b, 0), beta_blks)
    yield
    gr = _each(lambda gf: dot(m["ones3"], jnp.concatenate(_split3(gf[1] * m["eye"]), axis=0)), g_forms)
    yield
    gamma = _each(lambda gf, r: jnp.exp(jnp.where(m["incl"], gf[1] - r, -jnp.inf)), g_forms, gr)
    yield
    kb = _each(lambda k: k.astype(BF16), ks)
    yield
    qk_kk = _each(lambda q, k_b: lax.dot_general(jnp.concatenate([q.astype(BF16), k_b], axis=0),
                                                 _tile_rows_masked(k_b, m["bd_wide"]), _NT,
                                                 preferred_element_type=F32), qs, kb)
    yield
    qk = _each(lambda r, gm: (r[:c] * gm).astype(BF16), qk_kk, gamma)
    yield
    a = _each(lambda bf, r, gm: jnp.where(m["strict"], bf[1] * r[c:] * gm, 0.0), b_forms, qk_kk, gamma)
    yield
    x = _each(lambda a_: m["eye"] - a_, a)
    yield
    ab = _each(lambda a_: a_.astype(BF16), a)
    yield
    p = _each(lambda a_b: dot(a_b, _tile_rows_masked(a_b, m["bd_sq"])), ab)
    yield
    for it in range(5):
        pb = _each(lambda p_: p_.astype(BF16), p)
        yield
        p_bd = _each(lambda p_b: _tile_rows_masked(p_b, m["bd_sq"]), pb)
        yield
        if it < 4:
            both = _each(lambda x_, p_b, bd: dot(jnp.concatenate([x_.astype(BF16), p_b], axis=0), bd), x, pb, p_bd)
            x = _each(lambda x_, r: x_ + r[:c], x, both)
            p = _each(lambda r: r[c:], both)
        else:
            x = _each(lambda x_, bd: x_ + dot(x_.astype(BF16), bd), x, p_bd)
    xb = _each(lambda x_: x_.astype(BF16), x)
    yield
    eg = _each(lambda gf: jnp.exp(gf[0]), g_forms)
    yield
    u = _each(lambda x_b, v, bf: dot(x_b, _tile_rows_masked((v * bf[0]).astype(BF16), m["bd_wide"])),
              xb, vs, b_forms)
    yield
    w = _each(lambda x_b, k, bf, e: dot(x_b, _tile_rows_masked((k * (bf[0] * e)).astype(BF16), m["bd_wide"])),
              xb, ks, b_forms, eg)
    yield
    glast = _each(lambda g: jnp.concatenate(
        [jnp.broadcast_to(g[c - 1:c, H_A + h:H_A + h + 1], (1, DV_A)) for h in range(H_A)], axis=1), gcum)
    yield
    kd = _each(lambda k, gl, gf: (k * jnp.exp(gl - gf[0])).astype(BF16), ks, glast, g_forms)
    yield
    kd_stack = _each(lambda kd_: jnp.concatenate([kd_[:, h * DK_A:(h + 1) * DK_A] for h in range(H_A)], axis=0), kd)
    yield
    wq = _each(lambda w_, q, e: [
        jnp.concatenate([w_[:, h * DK_A:(h + 1) * DK_A], (q * e)[:, h * DK_A:(h + 1) * DK_A]], axis=0).astype(BF16)
        for h in range(H_A)], w, qs, eg)
    decay = _each(jnp.exp, glast)
    yield
    return list(zip(u, wq, qk, kd_stack, decay))


def _gdn_step(prep, s_ref, m):
    u, wq, qk, kd_stack, decay = prep
    c = CHUNK
    s = s_ref[...]
    sb = s.astype(BF16)
    v_new, o_state = [], []
    for h in range(H_A):
        hs = slice(h * DV_A, (h + 1) * DV_A)
        ws = jnp.dot(wq[h], sb[:, hs], preferred_element_type=F32)
        v_new.append(u[:, hs] - ws[:c])
        o_state.append(ws[c:])
    v_bd = _tile_rows_masked(jnp.concatenate(v_new, axis=1).astype(BF16), m["bd_wide"])
    o = jnp.concatenate(o_state, axis=1) + jnp.dot(qk, v_bd, preferred_element_type=F32)
    s_ref[...] = s * decay + lax.dot_general(kd_stack, v_bd, _TN, preferred_element_type=F32)
    return o


def _hgrn_prep(qs, ks, vs, lfs, m):
    c = CHUNK
    d = D_C
    dot = functools.partial(jnp.dot, preferred_element_type=F32)
    bc = _each(_cumsum_rows, lfs)
    yield
    att = [jnp.zeros((c, H_C * c), F32) for _ in qs]
    for s in (32, 16, 8):
        odd = m["odd", s]
        ref = _each(lambda b: jnp.concatenate(
            [jnp.broadcast_to(b[r:r + 1, :], (2 * s, d)) for r in range(s, c, 2 * s)], axis=0), bc)
        yield
        e = _each(lambda b, rf: jnp.exp(jnp.where(odd, b - rf, rf - b)), bc, ref)
        yield
        qt = _each(lambda q, e_: jnp.where(odd, q * e_, 0.0).astype(BF16), qs, e)
        yield
        kt = _each(lambda k, e_: jnp.where(odd, 0.0, k * e_).astype(BF16), ks, e)
        yield
        lvl = _each(lambda q_t, k_t: lax.dot_general(q_t, _tile_rows_masked(k_t, m["bd_sq"]), _NT,
                                                     preferred_element_type=F32), qt, kt)
        yield
        att = _each(lambda a_, l: a_ + jnp.where(m["pair", s], l, 0.0), att, lvl)
        yield

    def shifted_prods(q, k, b):
        out = []
        for o in range(SUBLANES):
            ko = k if o == 0 else pltpu.roll(k, o, 0)
            bo = b if o == 0 else pltpu.roll(b, o, 0)
            out.append((q * ko * jnp.exp(jnp.where(m["valid", o], b - bo, -jnp.inf))).astype(BF16))
        return jnp.concatenate(out, axis=0)

    prods = _each(shifted_prods, qs, ks, bc)
    yield
    dsum = _each(lambda pr: dot(pr, m["bd_sq"]), prods)
    yield
    for o in range(SUBLANES):
        att = _each(lambda a_, ds: a_ + jnp.where(m["diag", o], ds[o * c:(o + 1) * c], 0.0), att, dsum)
        yield
    vb = _each(lambda v: v.astype(BF16), vs)
    yield
    o_intra = _each(lambda a_, v_b: dot(a_.astype(BF16), _tile_rows_masked(v_b, m["bd_sq"])), att, vb)
    yield
    qdec = _each(lambda q, b: (q * jnp.exp(b)).astype(BF16), qs, bc)
    yield
    kdec = _each(lambda k, b: (k * jnp.exp(b[c - 1:c, :] - b)).astype(BF16), ks, bc)
    yield
    decay = _each(lambda b: jnp.exp(b[c - 1:c, :]), bc)
    yield
    return list(zip(o_intra, qdec, kdec, vb, decay))


def _hgrn_step(prep, st_ref, m):
    o_intra, qdec, kdec, vb, decay = prep
    st = st_ref[...]
    o = o_intra + lax.dot_general(qdec, st.astype(BF16), _NT, preferred_element_type=F32)
    upd = lax.dot_general(vb, kdec, _TN, preferred_element_type=F32)
    st_ref[...] = st * decay + upd * m["bd_sq"].astype(F32)
    return o


def _mix_tile(layer, tile, x_ref, y_ref, proj, cat, aconv_s, sa_s, bconv_s, bh_s, sc_s, w, masks):
    p_ref, win_ref, wgate_ref, wout_ref = w
    d = D_C
    x = x_ref[...]
    h_in = _rms(x, p_ref[MP_NORM:MP_NORM + 1, 0:D_MODEL]).astype(BF16)
    yield

    def in_proj(lo, width):
        return jnp.dot(h_in, win_ref[:, lo:lo + width], preferred_element_type=F32)

    for j in range(3 * D_A // FFN_BLK):
        lo = j * FFN_BLK
        pre = in_proj(lo, FFN_BLK)
        yield
        act = _silu(_causal_conv(pre, aconv_s[:, lo:lo + FFN_BLK], p_ref[MP_ACONV:MP_ACONV + A_CONV, lo:lo + FFN_BLK]))
        aconv_s[:, lo:lo + FFN_BLK] = pre[tile - SUBLANES:tile]
        if lo < 2 * D_A:
            for hh in range(FFN_BLK // DK_A):
                blk = act[:, hh * DK_A:(hh + 1) * DK_A]
                nrm = blk * lax.rsqrt(jnp.sum(blk * blk, axis=-1, keepdims=True) + 1e-6)
                if lo < D_A:
                    nrm = nrm * (DK_A ** -0.5)
                proj[:, lo + hh * DK_A:lo + (hh + 1) * DK_A] = nrm
        else:
            proj[:, lo:lo + FFN_BLK] = act
        yield
    ba = in_proj(OFF_BA, LANES)
    proj[:, OFF_G:OFF_G + LANES] = (-jnp.exp(p_ref[MP_ALOG:MP_ALOG + 1, 0:LANES])) * jax.nn.softplus(
        ba + p_ref[MP_DT:MP_DT + 1, 0:LANES])
    proj[:, OFF_BA:OFF_BA + LANES] = jax.nn.sigmoid(ba)
    yield

    bpre = in_proj(OFF_BX, D_B)
    yield
    xb = _causal_conv(bpre, bconv_s[...], p_ref[MP_BCONV:MP_BCONV + B_CONV, 0:D_B]) + p_ref[MP_BCONVB:MP_BCONVB + 1, 0:D_B]
    bconv_s[...] = bpre[tile - SUBLANES:tile]
    gates = _mm(xb, wgate_ref[...])
    r_gate = jax.nn.sigmoid(gates[:, :D_B] + p_ref[MP_BR:MP_BR + 1, 0:D_B])
    i_gate = jax.nn.sigmoid(gates[:, D_B:] + p_ref[MP_BI:MP_BI + 1, 0:D_B])
    yield
    log_a = (-RG_C) * r_gate * jax.nn.softplus(-p_ref[MP_LAMBDA:MP_LAMBDA + 1, 0:D_B])
    a_gate = jnp.exp(log_a)
    one_m_a2 = -jnp.tanh(log_a) * (a_gate * a_gate + 1.0)
    mult = jnp.where(one_m_a2 > 0.0, one_m_a2 * lax.rsqrt(one_m_a2), 0.0)
    yield
    for lo in (OFF_Z, OFF_Z + FFN_BLK, OFF_CI, OFF_CG):
        proj[:, lo:lo + FFN_BLK] = in_proj(lo, FFN_BLK)
        yield
    hb = _lin_scan(a_gate, mult * i_gate * xb, bh_s[...])
    bh_s[...] = hb[tile - 1:tile]
    yield
    cat[:, D_A:D_A + D_B] = hb * jax.nn.gelu(in_proj(OFF_BG, D_B))
    yield

    lg = p_ref[MP_CLB:MP_CLB + N_LAYERS, 0:d]
    ex = jnp.exp(lg - jnp.max(lg, axis=0, keepdims=True))
    sm = ex / jnp.sum(ex, axis=0, keepdims=True)
    lb = jnp.sum(sm[0:layer + 1], axis=0, keepdims=True) - sm[0:1]
    forget = lb + (1.0 - lb) * jax.nn.sigmoid(in_proj(OFF_CF, d))
    proj[:, OFF_CK:OFF_CK + d] = 1.0 - forget
    proj[:, OFF_CF:OFF_CF + d] = jnp.log(forget)
    yield
    proj[:, OFF_CQ:OFF_CQ + d] = _silu(in_proj(OFF_CQ, d))
    yield

    chunks = [slice(ci * CHUNK, (ci + 1) * CHUNK) for ci in range(tile // CHUNK)]

    def cols(lo, width):
        return [proj[rows, lo:lo + width] for rows in chunks]

    gdn = _gdn_prep(cols(0, D_A), cols(D_A, D_A), cols(2 * D_A, D_A), cols(OFF_BA, LANES), cols(OFF_G, LANES), masks)
    hgrn = _hgrn_prep(cols(OFF_CQ, d), cols(OFF_CK, d), cols(OFF_CI, d), cols(OFF_CF, d), masks)
    gdn_preps, hgrn_preps = yield from _interleave(gdn, hgrn)
    for rows, gdn_prep, hgrn_prep in zip(chunks, gdn_preps, hgrn_preps):
        cat[rows, 0:D_A] = _gdn_step(gdn_prep, sa_s, masks)
        yield
        cat[rows, D_A + D_B:D_MODEL] = _hgrn_step(hgrn_prep, sc_s, masks)
        yield

    for h in range(H_A):
        blk = cat[:, h * DV_A:(h + 1) * DV_A]
        ms = jnp.mean(blk * blk, axis=-1, keepdims=True)
        z = proj[:, OFF_Z + h * DV_A:OFF_Z + (h + 1) * DV_A]
        cat[:, h * DV_A:(h + 1) * DV_A] = (blk * lax.rsqrt(ms + RMS_EPS) * p_ref[MP_ANORM:MP_ANORM + 1, 0:DV_A]
                                           * _silu(z))
        yield
    oc = cat[:, D_A + D_B:D_MODEL]
    ms_c = _mm_sel(oc * oc, masks["bd_sq"], parts=2) * (1.0 / DK_C)
    cat[:, D_A + D_B:D_MODEL] = (oc * lax.rsqrt(ms_c + RMS_EPS) * p_ref[MP_CNORM:MP_CNORM + 1, 0:d]
                                 * _silu(proj[:, OFF_CG:OFF_CG + d]))
    yield
    y_ref[...] = x + _mm(cat[...], wout_ref[...])


def _mix_kernel(layer, zero_state, *refs):
    n_state_in = 0 if zero_state else 5
    x_ref = refs[0]
    state_in = refs[1:1 + n_state_in]
    w = refs[1 + n_state_in:5 + n_state_in]
    y_ref, aconv_o, as_o, bconv_o, bh_o, cs_o, proj, cat, aconv_s, sa_s, bconv_s, bh_s, sc_s = refs[5 + n_state_in:]
    t = pl.program_id(1)
    nt = pl.num_programs(1)
    n_inst, tile = x_ref.shape[0], x_ref.shape[1]
    d = D_C
    masks = _chunk_masks()

    @pl.when(t == 0)
    def _load_state():
        if zero_state:
            for ref in (aconv_s, sa_s, bconv_s, bh_s, sc_s):
                ref[...] = jnp.zeros(ref.shape, F32)
            return
        aconv_ref, as_ref, bconv_ref, bh_ref, cs_ref = state_in
        for i in range(n_inst):
            aconv_s[i] = aconv_ref[i]
            bconv_s[i] = bconv_ref[i]
            bh_s[i] = bh_ref[i]
            for h in range(H_A):
                sa_s[i, :, h * DV_A:(h + 1) * DV_A] = as_ref[i, h]
            rows = jnp.concatenate([cs_ref[i, h] for h in range(H_C)], axis=0)
            tile_sel = jnp.where(_iota((DK_C, d), 1) % DK_C == _iota((DK_C, d), 0), 1.0, 0.0).astype(BF16)
            s_bd = _mm_sel(rows, tile_sel) * masks["bd_sq"].astype(F32)
            sc_s[i] = s_bd.T

    _interleave_skewed(
        [_mix_tile(layer, tile, x_ref.at[i], y_ref.at[i], proj.at[i], cat.at[i], aconv_s.at[i], sa_s.at[i],
                   bconv_s.at[i], bh_s.at[i], sc_s.at[i], w, masks) for i in range(n_inst)], INST_SKEW)

    @pl.when(t == nt - 1)
    def _store_state():
        for i in range(n_inst):
            aconv_o[i] = aconv_s[i][SUBLANES - (A_CONV - 1):]
            bconv_o[i] = bconv_s[i][SUBLANES - (B_CONV - 1):]
            bh_o[i] = bh_s[i]
            for h in range(H_A):
                as_o[i, h] = sa_s[i, :, h * DV_A:(h + 1) * DV_A]
            s_bd = sc_s[i].T
            for h in range(H_C):
                sel = jnp.where(_iota((d, DK_C), 0) == _iota((d, DK_C), 1) + h * DK_C, 1.0, 0.0).astype(BF16)
                cs_o[i, h] = _mm_sel(s_bd[h * DK_C:(h + 1) * DK_C, :], sel)


def _ffn_kernel(final, zero_state, *refs):
    n_state_in = 0 if zero_state else 1
    x_ref, mk_ref, mv_ref = refs[:3]
    state_in = refs[3:3 + n_state_in]
    p_ref, wq_ref, wo_ref, wup_ref, wdown_ref, y_ref, fconv_o, kb_s, vb_s, carry_s, act_s = refs[3 + n_state_in:]
    t = pl.program_id(1)
    nt = pl.num_programs(1)
    tile = x_ref.shape[1]

    @pl.when(t == 0)
    def _load():
        for h in range(MEM_HEADS):
            kb_s[:, h * MEM_HEAD_DIM:(h + 1) * MEM_HEAD_DIM] = mk_ref[:, h, :].astype(BF16)
            vb_s[:, h * MEM_HEAD_DIM:(h + 1) * MEM_HEAD_DIM] = mv_ref[:, h, :].astype(BF16)
        if zero_state:
            carry_s[...] = jnp.zeros(carry_s.shape, F32)
        else:
            carry_s[...] = state_in[0][0]

    x = x_ref[0]
    hq = _rms(x, p_ref[FP_NORM_ATTN:FP_NORM_ATTN + 1, 0:D_MODEL]).astype(BF16)
    q = jnp.dot(hq, wq_ref[...], preferred_element_type=F32)
    head_cols = [slice(h * MEM_HEAD_DIM, (h + 1) * MEM_HEAD_DIM) for h in range(MEM_HEADS)]
    scores = [_mm_nt(q[:, sl], kb_s[:, sl]) * (MEM_HEAD_DIM ** -0.5) for sl in head_cols]
    expd = [jnp.exp(s - jnp.max(s, axis=-1, keepdims=True)) for s in scores]
    probs = [e / jnp.sum(e, axis=-1, keepdims=True) for e in expd]
    heads = [_mm(p, vb_s[:, sl]) for p, sl in zip(probs, head_cols)]
    x1 = x + _mm(jnp.concatenate(heads, axis=1), wo_ref[...])

    hf = _rms(x1, p_ref[FP_NORM_FFN:FP_NORM_FFN + 1, 0:D_MODEL]).astype(BF16)
    for j in range(N_FFN_BLK):
        halves = []
        for off in (j * FFN_BLK, D_FF + j * FFN_BLK):
            u = jnp.dot(hf, wup_ref[:, off:off + FFN_BLK], preferred_element_type=F32)
            halves.append(_causal_conv(u, carry_s[:, off:off + FFN_BLK],
                                       p_ref[FP_CONV:FP_CONV + FFN_CONV, off:off + FFN_BLK])
                          + p_ref[FP_CONVB:FP_CONVB + 1, off:off + FFN_BLK])
            carry_s[:, off:off + FFN_BLK] = u[tile - SUBLANES:tile]
        act_s[:, j * FFN_BLK:(j + 1) * FFN_BLK] = (_silu(halves[0]) * halves[1]).astype(BF16)
    y = x1 + jnp.dot(act_s[...], wdown_ref[...], preferred_element_type=F32)
    if final:
        y = _rms(y, p_ref[FP_NORM_FINAL:FP_NORM_FINAL + 1, 0:D_MODEL])
    y_ref[0] = y

    @pl.when(t == nt - 1)
    def _store():
        fconv_o[0] = carry_s[SUBLANES - (FFN_CONV - 1):, :]


def _memkv_kernel(mem_ref, g_ref, wk_ref, wv_ref, k_ref, v_ref):
    mn = _rms(mem_ref[...], g_ref[...]).astype(BF16)
    k = jnp.dot(mn, wk_ref[...], preferred_element_type=F32)
    v = jnp.dot(mn, wv_ref[...], preferred_element_type=F32)
    for h in range(MEM_HEADS):
        k_ref[:, h, :] = k[:, h * MEM_HEAD_DIM:(h + 1) * MEM_HEAD_DIM]
        v_ref[:, h, :] = v[:, h * MEM_HEAD_DIM:(h + 1) * MEM_HEAD_DIM]


def _layer_resident(layer, shape):
    nd = len(shape)
    return pl.BlockSpec((None,) + tuple(shape[1:]), lambda b, t: (layer,) + (0,) * (nd - 1),
                        pipeline_mode=pl.Buffered(1))


def _layer_rows(layer, shape, n_rows):
    nd = len(shape)
    return pl.BlockSpec((None, n_rows) + tuple(shape[2:]), lambda g, t: (layer, g) + (0,) * (nd - 2))


def _rows_out(shape, n_rows):
    nd = len(shape)
    return pl.BlockSpec((n_rows,) + tuple(shape[1:]), lambda g, t: (g,) + (0,) * (nd - 1))


def _seq_tile(length):
    return 256 if length % 256 == 0 else CHUNK


def _mix_call(layer, x, states, wts):
    bsz, length, _ = x.shape
    tile = _seq_tile(length)
    x_spec = pl.BlockSpec((N_INST, tile, D_MODEL), lambda g, t: (g, t, 0))
    state_shapes = [(bsz, A_CONV - 1, 3 * D_A), (bsz, H_A, DK_A, DV_A), (bsz, B_CONV - 1, D_B), (bsz, 1, D_B),
                    (bsz, H_C, DK_C, DK_C)]
    state_in = [] if states is None else list(states)
    return pl.pallas_call(
        functools.partial(_mix_kernel, layer, states is None),
        grid=(bsz // N_INST, length // tile),
        in_specs=[x_spec] + [_layer_rows(layer, s.shape, N_INST) for s in state_in]
        + [_layer_resident(layer, w.shape) for w in wts],
        out_specs=[x_spec] + [_rows_out(shp, N_INST) for shp in state_shapes],
        out_shape=[jax.ShapeDtypeStruct(x.shape, F32)] + [jax.ShapeDtypeStruct(shp, F32) for shp in state_shapes],
        scratch_shapes=[
            pltpu.VMEM((N_INST, tile, PROJ_W), F32),
            pltpu.VMEM((N_INST, tile, D_MODEL), F32),
            pltpu.VMEM((N_INST, SUBLANES, 3 * D_A), F32),
            pltpu.VMEM((N_INST, DK_A, H_A * DV_A), F32),
            pltpu.VMEM((N_INST, SUBLANES, D_B), F32),
            pltpu.VMEM((N_INST, 1, D_B), F32),
            pltpu.VMEM((N_INST, D_C, D_C), F32),
        ],
        compiler_params=pltpu.CompilerParams(
            dimension_semantics=("arbitrary", "arbitrary"), vmem_limit_bytes=VMEM_LIMIT_BYTES),
        name="mix_block",
    )(x, *state_in, *wts)


def _ffn_call(layer, final, x, mem_k, mem_v, fconv, wts):
    bsz, length, _ = x.shape
    tile = _seq_tile(length)
    x_spec = pl.BlockSpec((1, tile, D_MODEL), lambda b, t: (b, t, 0))
    kv_spec = pl.BlockSpec((None, None) + tuple(mem_k.shape[2:]), lambda b, t: (layer, b, 0, 0, 0))
    state_in = [] if fconv is None else [fconv]
    fconv_shape = (bsz, FFN_CONV - 1, 2 * D_FF)
    return pl.pallas_call(
        functools.partial(_ffn_kernel, final, fconv is None),
        grid=(bsz, length // tile),
        in_specs=[x_spec, kv_spec, kv_spec] + [_layer_rows(layer, s.shape, 1) for s in state_in]
        + [_layer_resident(layer, w.shape) for w in wts],
        out_specs=[x_spec, _rows_out(fconv_shape, 1)],
        out_shape=[jax.ShapeDtypeStruct(x.shape, F32), jax.ShapeDtypeStruct(fconv_shape, F32)],
        scratch_shapes=[
            pltpu.VMEM((N_MEM, D_MODEL), BF16),
            pltpu.VMEM((N_MEM, D_MODEL), BF16),
            pltpu.VMEM((SUBLANES, 2 * D_FF), F32),
            pltpu.VMEM((tile, D_FF), BF16),
        ],
        compiler_params=pltpu.CompilerParams(
            dimension_semantics=("arbitrary", "arbitrary"), vmem_limit_bytes=VMEM_LIMIT_BYTES),
        name="attn_ffn_block",
    )(x, mem_k, mem_v, *state_in, *wts)


def _memkv_call(mem, norm_mem, wk, wv):
    depth = wk.shape[0]
    bsz, n_mem, _ = mem.shape
    w_spec = pl.BlockSpec((None, D_MODEL, D_MODEL), lambda l, b: (l, 0, 0))
    o_spec = pl.BlockSpec((None, None, n_mem, MEM_HEADS, MEM_HEAD_DIM), lambda l, b: (l, b, 0, 0, 0))
    shape = jax.ShapeDtypeStruct((depth, bsz, n_mem, MEM_HEADS, MEM_HEAD_DIM), F32)
    return pl.pallas_call(
        _memkv_kernel,
        grid=(depth, bsz),
        in_specs=[pl.BlockSpec((None, n_mem, D_MODEL), lambda l, b: (b, 0, 0)),
                  pl.BlockSpec((None, 1, D_MODEL), lambda l, b: (l, 0, 0)), w_spec, w_spec],
        out_specs=[o_spec, o_spec],
        out_shape=[shape, shape],
        compiler_params=pltpu.CompilerParams(
            dimension_semantics=("arbitrary", "arbitrary"), vmem_limit_bytes=VMEM_LIMIT_BYTES),
        name="memory_kv",
    )(mem, norm_mem.reshape(depth, 1, D_MODEL), wk, wv)


def _pack_rows(pieces, width):
    padded = [jnp.pad(p, ((0, 0), (0, 0), (0, width - p.shape[2]))) for p in pieces]
    rows = sum(p.shape[1] for p in pieces)
    fill = jnp.zeros((pieces[0].shape[0], -rows % SUBLANES, width), F32)
    return jnp.concatenate(padded + [fill], axis=1)


def kernel(x_prompt, x_sample, cache_mem_k, cache_mem_v, state_a_conv, state_a_S, state_b_conv, state_b_h, state_c_S, state_ffn_conv, mem_prompt, norm_mix, w_in, a_conv_w, a_A_log, a_dt_bias, a_norm, b_conv_w, b_conv_b, b_w_r, b_b_r, b_w_i, b_b_i, b_lambda, c_lb_logits, c_norm, w_out, norm_attn, norm_mem, w_q, w_k, w_v, w_o, norm_ffn, w_up, ffn_conv_w, ffn_conv_b, w_down, norm_final):
    depth = w_in.shape[0]
    assert depth == N_LAYERS and x_prompt.shape[0] % N_INST == 0 and x_sample.shape[0] % N_INST == 0

    def rows(v):
        return v[:, None, :]

    w_in_pad = jnp.concatenate(
        [w_in[:, :, :OFF_BA + 2 * H_A], jnp.zeros((depth, D_MODEL, LANES - 2 * H_A), F32), w_in[:, :, OFF_BA + 2 * H_A:]],
        axis=2).astype(BF16)
    eye = jnp.eye(B_BLOCKS, dtype=F32)
    w_gate = jnp.concatenate(
        [jnp.einsum('lnij,nm->lnimj', wg, eye).reshape(depth, D_B, D_B) for wg in (b_w_r, b_w_i)], axis=2).astype(BF16)
    lane_pad = jnp.zeros((depth, 1, H_A), F32)
    mix_p = _pack_rows([
        a_conv_w, rows(norm_mix), jnp.concatenate([lane_pad, rows(a_A_log)], axis=2),
        jnp.concatenate([lane_pad, rows(a_dt_bias)], axis=2), rows(a_norm),
        b_conv_w, rows(b_conv_b), rows(b_b_r), rows(b_b_i), rows(b_lambda), rows(jnp.tile(c_norm, (1, H_C))),
        jnp.broadcast_to(c_lb_logits[None], (depth,) + c_lb_logits.shape)], 3 * D_A)
    ffn_p = _pack_rows([
        ffn_conv_w, rows(ffn_conv_b), rows(norm_attn), rows(norm_ffn),
        jnp.broadcast_to(norm_final[None, None, :], (depth, 1, D_MODEL))], 2 * D_FF)
    mix_w = [mix_p, w_in_pad, w_gate, w_out.astype(BF16)]
    ffn_w = [ffn_p, w_q.astype(BF16), w_o.astype(BF16), w_up.astype(BF16), w_down.astype(BF16)]

    p_mem_k, p_mem_v = _memkv_call(mem_prompt, norm_mem, w_k.astype(BF16), w_v.astype(BF16))

    def pad8(s):
        return jnp.pad(s, ((0, 0), (0, 0), (SUBLANES - s.shape[2], 0), (0, 0)))

    def run_group(x, mem_k, mem_v, states):
        if states is not None:
            a_conv, a_s, b_conv, b_h, c_s, f_conv = states
            mix_states = [pad8(a_conv), a_s, pad8(b_conv), b_h[:, :, None, :], c_s]
            f_conv = pad8(f_conv)
        else:
            mix_states = f_conv = None
        outs = []
        for l in range(depth):
            x, na_conv, na_s, nb_conv, nb_h, nc_s = _mix_call(l, x, mix_states, mix_w)
            x, nf_conv = _ffn_call(l, l == depth - 1, x, mem_k, mem_v, f_conv, ffn_w)
            outs.append((na_conv, na_s, nb_conv, nb_h[:, 0], nc_s, nf_conv))
        return x, [jnp.stack(t, axis=0) for t in zip(*outs)]

    y_prompt, p_states = run_group(x_prompt, p_mem_k, p_mem_v, None)
    y_sample, s_states = run_group(
        x_sample, cache_mem_k, cache_mem_v,
        (state_a_conv, state_a_S, state_b_conv, state_b_h, state_c_S, state_ffn_conv))
    return (y_prompt, y_sample, p_mem_k, p_mem_v, *p_states, *s_states)
```

```python
import functools
import math

import jax
import jax.numpy as jnp
from jax import lax
from jax.experimental import pallas as pl
from jax.experimental.pallas import tpu as pltpu

F32 = jnp.float32
BF16 = jnp.bfloat16

D_MODEL = 1024
CHUNK = 64
D_A = 512
DK_A = 128
DV_A = 128
H_A = 4
A_CONV = 4
D_B = 256
B_BLOCKS = 4
B_CONV = 4
RG_C = 8.0
D_C = 256
H_C = 4
DK_C = 64
N_MEM = 256
MEM_HEADS = 4
MEM_HEAD_DIM = 256
D_FF = 2816
FFN_CONV = 3
RMS_EPS = 1e-6

LANES = 128
SUBLANES = 8
VMEM_LIMIT_BYTES = 56 * 1024 * 1024

OFF_QKV = 0
OFF_Z = 1536
OFF_BA = 2048
OFF_BX = 2176
OFF_BG = 2432
OFF_CQ = 2688
OFF_CF = 2944
OFF_CI = 3200
OFF_CG = 3456
N_PROJ = 3712
OFF_CK = 3712
OFF_G = 3968
PROJ_W = 4096
FFN_BLK = 256
N_FFN_BLK = D_FF // FFN_BLK
FFN_TILE = 512

N_LAYERS = 2

MP_ACONV, MP_NORM, MP_ALOG, MP_DT, MP_ANORM = 0, 4, 5, 6, 7
MP_BCONV, MP_BCONVB, MP_BR, MP_BI, MP_LAMBDA, MP_CNORM, MP_CLB = 8, 12, 13, 14, 15, 16, 17
FP_CONV, FP_CONVB, FP_NORM_ATTN, FP_NORM_FFN, FP_NORM_FINAL = 0, 3, 4, 5, 6

N_INST = 2
INST_SKEW = 12

_NT = (((1,), (1,)), ((), ()))
_TN = (((0,), (0,)), ((), ()))


def _mm(a, b):
    return jnp.dot(a.astype(BF16), b.astype(BF16), preferred_element_type=F32)


def _mm_nt(a, b):
    return lax.dot_general(a.astype(BF16), b.astype(BF16), _NT, preferred_element_type=F32)


def _split3(x):
    hi = x.astype(BF16)
    r = x - hi.astype(F32)
    mid = r.astype(BF16)
    lo = (r - mid.astype(F32)).astype(BF16)
    return hi, mid, lo


def _mm_sel(x, sel, parts=3):
    return sum(jnp.dot(part, sel, preferred_element_type=F32) for part in _split3(x)[:parts])


def _cumsum_rows(x):
    c = x.shape[0]
    tri3 = ((_iota((c, 3 * c), 1) % c) <= _iota((c, 3 * c), 0)).astype(BF16)
    return jnp.dot(tri3, jnp.concatenate(_split3(x), axis=0), preferred_element_type=F32)


def _rms(x, g):
    ms = jnp.mean(x * x, axis=-1, keepdims=True)
    return x * lax.rsqrt(ms + RMS_EPS) * g


def _silu(x):
    return x * jax.nn.sigmoid(x)


def _iota(shape, axis):
    return lax.broadcasted_iota(jnp.int32, shape, axis)


def _shift_rows(x, carry8, s):
    rolled = pltpu.roll(x, s, 0)
    head = jnp.where(_iota(carry8.shape, 0) < s, pltpu.roll(carry8, s, 0), rolled[0:SUBLANES])
    if x.shape[0] == SUBLANES:
        return head
    return jnp.concatenate([head, rolled[SUBLANES:]], axis=0)


def _causal_conv(x, carry8, w):
    width = w.shape[0]
    y = x * w[width - 1:width, :]
    for s in range(1, width):
        y = y + _shift_rows(x, carry8, s) * w[width - 1 - s:width - s, :]
    return y


def _roll_in_groups(x, s):
    n, width = x.shape
    return pltpu.roll(x.reshape(n // SUBLANES, SUBLANES, width), s, 1).reshape(n, width)


def _lin_scan(a, u, h0):
    n = a.shape[0]
    row8 = _iota(a.shape, 0) % SUBLANES
    s = 1
    while s < SUBLANES:
        keep = row8 >= s
        a_s = jnp.where(keep, _roll_in_groups(a, s), 1.0)
        u_s = jnp.where(keep, _roll_in_groups(u, s), 0.0)
        u = a * u_s + u
        a = a * a_s
        s *= 2
    groups = []
    carry = h0
    for r in range(0, n, SUBLANES):
        groups.append(a[r:r + SUBLANES] * carry + u[r:r + SUBLANES])
        carry = groups[-1][SUBLANES - 1:SUBLANES]
    return jnp.concatenate(groups, axis=0)


def _tile_rows_masked(x, mask):
    return jnp.concatenate([x] * (mask.shape[0] // x.shape[0]), axis=0) * mask


def _chunk_masks():
    c = CHUNK
    n = H_A * c
    rowi = _iota((c, n), 0)
    lane_j = _iota((c, n), 1) % c
    m = {
        "half": _iota((c, LANES), 1) < c,
        "incl": lane_j <= rowi,
        "strict": lane_j < rowi,
        "eye": jnp.where(lane_j == rowi, 1.0, 0.0),
        "bd_sq": jnp.where(_iota((n, n), 0) // c == _iota((n, n), 1) // c, 1.0, 0.0).astype(BF16),
        "bd_wide": jnp.where(_iota((n, H_A * DV_A), 0) // c == _iota((n, H_A * DV_A), 1) // DV_A,
                             1.0, 0.0).astype(BF16),
        "ones3": jnp.ones((c, 3 * c), BF16),
    }
    for s in (32, 16, 8):
        m["odd", s] = ((_iota((c, D_C), 0) // s) % 2) == 1
        m["pair", s] = (rowi // (2 * s)) == (lane_j // (2 * s))
    for o in range(SUBLANES):
        m["valid", o] = (_iota((c, D_C), 0) % SUBLANES) >= o
        m["diag", o] = lane_j == rowi - o
    return m


def _each(fn, *lists):
    return [fn(*args) for args in zip(*lists)]


def _interleave(*gens):
    results = [None] * len(gens)
    live = dict(enumerate(gens))
    while live:
        for i, g in list(live.items()):
            try:
                next(g)
            except StopIteration as done:
                results[i] = done.value
                del live[i]
        yield
    return results


def _interleave_skewed(gens, skew):
    live = list(gens)
    step = 0
    while live:
        for k, g in enumerate(gens):
            if g in live and step >= k * skew:
                try:
                    next(g)
                except StopIteration:
                    live.remove(g)
        step += 1


def _gdn_prep(qs, ks, vs, beta_blks, g_blks, m):
    c = CHUNK
    dot = functools.partial(jnp.dot, preferred_element_type=F32)
    gcum = _each(_cumsum_rows, g_blks)
    yield

    def lane_forms(blk, first):
        cols = [jnp.broadcast_to(blk[:, first + h:first + h + 1], (c, LANES)) for h in range(H_A)]
        wide = jnp.concatenate(cols, axis=1)
        sq = jnp.concatenate([jnp.where(m["half"], cols[0], cols[1]), jnp.where(m["half"], cols[2], cols[3])],
                             axis=1)
        return wide, sq

    g_forms = _each(lambda g: lane_forms(g, H_A), gcum)
    yield
    b_forms = _each(lambda b: lane_forms(b, 0), beta_blks)
    yield
    gr = _each(lambda gf: dot(m["ones3"], jnp.concatenate(_split3(gf[1] * m["eye"]), axis=0)), g_forms)
    yield
    gamma = _each(lambda gf, r: jnp.exp(jnp.where(m["incl"], gf[1] - r, -jnp.inf)), g_forms, gr)
    yield
    kb = _each(lambda k: k.astype(BF16), ks)
    yield
    qk_kk = _each(lambda q, k_b: lax.dot_general(jnp.concatenate([q.astype(BF16), k_b], axis=0),
                                                 _tile_rows_masked(k_b, m["bd_wide"]), _NT,
                                                 preferred_element_type=F32), qs, kb)
    yield
    qk = _each(lambda r, gm: (r[:c] * gm).astype(BF16), qk_kk, gamma)
    yield
    a = _each(lambda bf, r, gm: jnp.where(m["strict"], bf[1] * r[c:] * gm, 0.0), b_forms, qk_kk, gamma)
    yield
    x = _each(lambda a_: m["eye"] - a_, a)
    yield
    ab = _each(lambda a_: a_.astype(BF16), a)
    yield
    p = _each(lambda a_b: dot(a_b, _tile_rows_masked(a_b, m["bd_sq"])), ab)
    yield
    for it in range(5):
        pb = _each(lambda p_: p_.astype(BF16), p)
        yield
        p_bd = _each(lambda p_b: _tile_rows_masked(p_b, m["bd_sq"]), pb)
        yield
        if it < 4:
            both = _each(lambda x_, p_b, bd: dot(jnp.concatenate([x_.astype(BF16), p_b], axis=0), bd), x, pb, p_bd)
            x = _each(lambda x_, r: x_ + r[:c], x, both)
            p = _each(lambda r: r[c:], both)
        else:
            x = _each(lambda x_, bd: x_ + dot(x_.astype(BF16), bd), x, p_bd)
    xb = _each(lambda x_: x_.astype(BF16), x)
    yield
    eg = _each(lambda gf: jnp.exp(gf[0]), g_forms)
    yield
    u = _each(lambda x_b, v, bf: dot(x_b, _tile_rows_masked((v * bf[0]).astype(BF16), m["bd_wide"])),
              xb, vs, b_forms)
    yield
    w = _each(lambda x_b, k, bf, e: dot(x_b, _tile_rows_masked((k * (bf[0] * e)).astype(BF16), m["bd_wide"])),
              xb, ks, b_forms, eg)
    yield
    glast = _each(lambda g: jnp.concatenate(
        [jnp.broadcast_to(g[c - 1:c, H_A + h:H_A + h + 1], (1, DV_A)) for h in range(H_A)], axis=1), gcum)
    yield
    kd = _each(lambda k, gl, gf: (k * jnp.exp(gl - gf[0])).astype(BF16), ks, glast, g_forms)
    yield
    kd_stack = _each(lambda kd_: jnp.concatenate([kd_[:, h * DK_A:(h + 1) * DK_A] for h in range(H_A)], axis=0), kd)
    yield
    wq = _each(lambda w_, q, e: [
        jnp.concatenate([w_[:, h * DK_A:(h + 1) * DK_A], (q * e)[:, h * DK_A:(h + 1) * DK_A]], axis=0).astype(BF16)
        for h in range(H_A)], w, qs, eg)
    decay = _each(jnp.exp, glast)
    yield
    return list(zip(u, wq, qk, kd_stack, decay))


def _gdn_step(prep, s_ref, m):
    u, wq, qk, kd_stack, decay = prep
    c = CHUNK
    s = s_ref[...]
    sb = s.astype(BF16)
    v_new, o_state = [], []
    for h in range(H_A):
        hs = slice(h * DV_A, (h + 1) * DV_A)
        ws = jnp.dot(wq[h], sb[:, hs], preferred_element_type=F32)
        v_new.append(u[:, hs] - ws[:c])
        o_state.append(ws[c:])
    v_bd = _tile_rows_masked(jnp.concatenate(v_new, axis=1).astype(BF16), m["bd_wide"])
    o = jnp.concatenate(o_state, axis=1) + jnp.dot(qk, v_bd, preferred_element_type=F32)
    s_ref[...] = s * decay + lax.dot_general(kd_stack, v_bd, _TN, preferred_element_type=F32)
    return o


def _hgrn_prep(qs, ks, vs, lfs, m):
    c = CHUNK
    d = D_C
    dot = functools.partial(jnp.dot, preferred_element_type=F32)
    bc = _each(_cumsum_rows, lfs)
    yield
    att = [jnp.zeros((c, H_C * c), F32) for _ in qs]
    for s in (32, 16, 8):
        odd = m["odd", s]
        ref = _each(lambda b: jnp.concatenate(
            [jnp.broadcast_to(b[r:r + 1, :], (2 * s, d)) for r in range(s, c, 2 * s)], axis=0), bc)
        yield
        e = _each(lambda b, rf: jnp.exp(jnp.where(odd, b - rf, rf - b)), bc, ref)
        yield
        def keep_blocks(x, want_odd):
            zero = jnp.zeros((s, d), F32)
            return jnp.concatenate(
                [x[r:r + s] if ((r // s) % 2 == 1) == want_odd else zero for r in range(0, c, s)], axis=0)

        qt = _each(lambda q, e_: keep_blocks(q * e_, True).astype(BF16), qs, e)
        yield
        kt = _each(lambda k, e_: keep_blocks(k * e_, False).astype(BF16), ks, e)
        yield
        lvl = _each(lambda q_t, k_t: lax.dot_general(q_t, _tile_rows_masked(k_t, m["bd_sq"]), _NT,
                                                     preferred_element_type=F32), qt, kt)
        yield
        att = _each(lambda a_, l: a_ + jnp.where(m["pair", s], l, 0.0), att, lvl)
        yield

    def shifted_prods(q, k, b):
        out = []
        for o in range(SUBLANES):
            ko = k if o == 0 else _roll_in_groups(k, o)
            bo = b if o == 0 else _roll_in_groups(b, o)
            out.append((q * ko * jnp.exp(jnp.where(m["valid", o], b - bo, -jnp.inf))).astype(BF16))
        return jnp.concatenate(out, axis=0)

    prods = _each(shifted_prods, qs, ks, bc)
    yield
    dsum = _each(lambda pr: dot(pr, m["bd_sq"]), prods)
    yield
    for o in range(SUBLANES):
        att = _each(lambda a_, ds: a_ + jnp.where(m["diag", o], ds[o * c:(o + 1) * c], 0.0), att, dsum)
        yield
    vb = _each(lambda v: v.astype(BF16), vs)
    yield
    o_intra = _each(lambda a_, v_b: dot(a_.astype(BF16), _tile_rows_masked(v_b, m["bd_sq"])), att, vb)
    yield
    qdec = _each(lambda q, b: (q * jnp.exp(b)).astype(BF16), qs, bc)
    yield
    kdec = _each(lambda k, b: (k * jnp.exp(b[c - 1:c, :] - b)).astype(BF16), ks, bc)
    yield
    decay = _each(lambda b: jnp.exp(b[c - 1:c, :]), bc)
    yield
    return list(zip(o_intra, qdec, kdec, vb, decay))


def _hgrn_step(prep, st_ref, m):
    o_intra, qdec, kdec, vb, decay = prep
    st = st_ref[...]
    o = o_intra + lax.dot_general(qdec, st.astype(BF16), _NT, preferred_element_type=F32)
    upd = lax.dot_general(vb, kdec, _TN, preferred_element_type=F32)
    st_ref[...] = st * decay + upd * m["bd_sq"].astype(F32)
    return o


def _mix_tile(layer, tile, x_ref, y_ref, proj, cat, aconv_s, sa_s, bconv_s, bh_s, sc_s, w, masks):
    p_ref, win_a_ref, win_ba_ref, win_c_ref, wgate_ref, wout_ref = w
    d = D_C
    x = x_ref[...]
    h_in = _rms(x, p_ref[MP_NORM:MP_NORM + 1, 0:D_MODEL]).astype(BF16)
    yield

    def in_proj(lo, width):
        if lo < OFF_BA:
            w_cols = win_a_ref[:, lo:lo + width]
        elif lo == OFF_BA:
            w_cols = win_ba_ref[...]
        else:
            w_cols = win_c_ref[:, lo - OFF_BX:lo - OFF_BX + width]
        return jnp.dot(h_in, w_cols, preferred_element_type=F32)

    for j in range(3 * D_A // FFN_BLK):
        lo = j * FFN_BLK
        pre = in_proj(lo, FFN_BLK)
        yield
        act = _silu(_causal_conv(pre, aconv_s[:, lo:lo + FFN_BLK], p_ref[MP_ACONV:MP_ACONV + A_CONV, lo:lo + FFN_BLK]))
        aconv_s[:, lo:lo + FFN_BLK] = pre[tile - SUBLANES:tile]
        if lo < 2 * D_A:
            for hh in range(FFN_BLK // DK_A):
                blk = act[:, hh * DK_A:(hh + 1) * DK_A]
                nrm = blk * lax.rsqrt(jnp.sum(blk * blk, axis=-1, keepdims=True) + 1e-6)
                if lo < D_A:
                    nrm = nrm * (DK_A ** -0.5)
                proj[:, lo + hh * DK_A:lo + (hh + 1) * DK_A] = nrm
        else:
            proj[:, lo:lo + FFN_BLK] = act
        yield
    ba = in_proj(OFF_BA, LANES)
    proj[:, OFF_G:OFF_G + LANES] = (-jnp.exp(p_ref[MP_ALOG:MP_ALOG + 1, 0:LANES])) * jax.nn.softplus(
        ba + p_ref[MP_DT:MP_DT + 1, 0:LANES])
    proj[:, OFF_BA:OFF_BA + LANES] = jax.nn.sigmoid(ba)
    yield

    bpre = in_proj(OFF_BX, D_B)
    yield
    xb = _causal_conv(bpre, bconv_s[...], p_ref[MP_BCONV:MP_BCONV + B_CONV, 0:D_B]) + p_ref[MP_BCONVB:MP_BCONVB + 1, 0:D_B]
    bconv_s[...] = bpre[tile - SUBLANES:tile]
    gates = _mm(xb, wgate_ref[...])
    r_gate = jax.nn.sigmoid(gates[:, :D_B] + p_ref[MP_BR:MP_BR + 1, 0:D_B])
    i_gate = jax.nn.sigmoid(gates[:, D_B:] + p_ref[MP_BI:MP_BI + 1, 0:D_B])
    yield
    log_a = (-RG_C) * r_gate * jax.nn.softplus(-p_ref[MP_LAMBDA:MP_LAMBDA + 1, 0:D_B])
    a_gate = jnp.exp(log_a)
    one_m_a2 = -jnp.tanh(log_a) * (a_gate * a_gate + 1.0)
    mult = jnp.where(one_m_a2 > 0.0, one_m_a2 * lax.rsqrt(one_m_a2), 0.0)
    yield
    for lo in (OFF_Z, OFF_Z + FFN_BLK, OFF_CI, OFF_CG):
        proj[:, lo:lo + FFN_BLK] = in_proj(lo, FFN_BLK)
        yield
    hb = _lin_scan(a_gate, mult * i_gate * xb, bh_s[...])
    bh_s[...] = hb[tile - 1:tile]
    yield
    cat[:, D_A:D_A + D_B] = hb * jax.nn.gelu(in_proj(OFF_BG, D_B))
    yield

    lg = p_ref[MP_CLB:MP_CLB + N_LAYERS, 0:d]
    ex = jnp.exp(lg - jnp.max(lg, axis=0, keepdims=True))
    sm = ex / jnp.sum(ex, axis=0, keepdims=True)
    lb = jnp.sum(sm[0:layer + 1], axis=0, keepdims=True) - sm[0:1]
    forget = lb + (1.0 - lb) * jax.nn.sigmoid(in_proj(OFF_CF, d))
    proj[:, OFF_CK:OFF_CK + d] = 1.0 - forget
    proj[:, OFF_CF:OFF_CF + d] = jnp.log(forget)
    yield
    proj[:, OFF_CQ:OFF_CQ + d] = _silu(in_proj(OFF_CQ, d))
    yield

    chunks = [slice(ci * CHUNK, (ci + 1) * CHUNK) for ci in range(tile // CHUNK)]

    def cols(lo, width):
        return [proj[rows, lo:lo + width] for rows in chunks]

    gdn = _gdn_prep(cols(0, D_A), cols(D_A, D_A), cols(2 * D_A, D_A), cols(OFF_BA, LANES), cols(OFF_G, LANES), masks)
    hgrn = _hgrn_prep(cols(OFF_CQ, d), cols(OFF_CK, d), cols(OFF_CI, d), cols(OFF_CF, d), masks)
    gdn_preps, hgrn_preps = yield from _interleave(gdn, hgrn)
    for rows, gdn_prep, hgrn_prep in zip(chunks, gdn_preps, hgrn_preps):
        cat[rows, 0:D_A] = _gdn_step(gdn_prep, sa_s, masks)
        yield
        cat[rows, D_A + D_B:D_MODEL] = _hgrn_step(hgrn_prep, sc_s, masks)
        yield

    for h in range(H_A):
        blk = cat[:, h * DV_A:(h + 1) * DV_A]
        ms = jnp.mean(blk * blk, axis=-1, keepdims=True)
        z = proj[:, OFF_Z + h * DV_A:OFF_Z + (h + 1) * DV_A]
        cat[:, h * DV_A:(h + 1) * DV_A] = (blk * lax.rsqrt(ms + RMS_EPS) * p_ref[MP_ANORM:MP_ANORM + 1, 0:DV_A]
                                           * _silu(z))
        yield
    oc = cat[:, D_A + D_B:D_MODEL]
    ms_c = _mm_sel(oc * oc, masks["bd_sq"], parts=2) * (1.0 / DK_C)
    cat[:, D_A + D_B:D_MODEL] = (oc * lax.rsqrt(ms_c + RMS_EPS) * p_ref[MP_CNORM:MP_CNORM + 1, 0:d]
                                 * _silu(proj[:, OFF_CG:OFF_CG + d]))
    yield
    y_ref[...] = x + _mm(cat[...], wout_ref[...])


def _mix_kernel(layer, zero_state, *refs):
    n_state_in = 0 if zero_state else 5
    x_ref = refs[0]
    state_in = refs[1:1 + n_state_in]
    w = refs[1 + n_state_in:7 + n_state_in]
    y_ref, aconv_o, as_o, bconv_o, bh_o, cs_o, proj, cat, aconv_s, sa_s, bconv_s, bh_s, sc_s = refs[7 + n_state_in:]
    t = pl.program_id(1)
    nt = pl.num_programs(1)
    n_inst, tile = x_ref.shape[0], x_ref.shape[1]
    d = D_C
    masks = _chunk_masks()

    @pl.when(t == 0)
    def _load_state():
        if zero_state:
            for ref in (aconv_s, sa_s, bconv_s, bh_s, sc_s):
                ref[...] = jnp.zeros(ref.shape, F32)
            return
        aconv_ref, as_ref, bconv_ref, bh_ref, cs_ref = state_in
        for i in range(n_inst):
            aconv_s[i] = aconv_ref[i]
            bconv_s[i] = bconv_ref[i]
            bh_s[i] = bh_ref[i]
            for h in range(H_A):
                sa_s[i, :, h * DV_A:(h + 1) * DV_A] = as_ref[i, h]
            rows = jnp.concatenate([cs_ref[i, h] for h in range(H_C)], axis=0)
            tile_sel = jnp.where(_iota((DK_C, d), 1) % DK_C == _iota((DK_C, d), 0), 1.0, 0.0).astype(BF16)
            s_bd = _mm_sel(rows, tile_sel) * masks["bd_sq"].astype(F32)
            sc_s[i] = s_bd.T

    _interleave_skewed(
        [_mix_tile(layer, tile, x_ref.at[i], y_ref.at[i], proj.at[i], cat.at[i], aconv_s.at[i], sa_s.at[i],
                   bconv_s.at[i], bh_s.at[i], sc_s.at[i], w, masks) for i in range(n_inst)], INST_SKEW)

    @pl.when(t == nt - 1)
    def _store_state():
        for i in range(n_inst):
            aconv_o[i] = aconv_s[i][SUBLANES - (A_CONV - 1):]
            bconv_o[i] = bconv_s[i][SUBLANES - (B_CONV - 1):]
            bh_o[i] = bh_s[i]
            for h in range(H_A):
                as_o[i, h] = sa_s[i, :, h * DV_A:(h + 1) * DV_A]
            s_bd = sc_s[i].T
            for h in range(H_C):
                sel = jnp.where(_iota((d, DK_C), 0) == _iota((d, DK_C), 1) + h * DK_C, 1.0, 0.0).astype(BF16)
                cs_o[i, h] = _mm_sel(s_bd[h * DK_C:(h + 1) * DK_C, :], sel)


def _ffn_kernel(final, zero_state, *refs):
    n_state_in = 0 if zero_state else 1
    x_ref, mk_ref, mv_ref = refs[:3]
    state_in = refs[3:3 + n_state_in]
    p_ref, wq_ref, wo_ref, wup_ref, wdown_ref, y_ref, fconv_o, kb_s, vb_s, carry_s, act_s = refs[3 + n_state_in:]
    t = pl.program_id(1)
    nt = pl.num_programs(1)
    tile = x_ref.shape[1]

    @pl.when(t == 0)
    def _load():
        for h in range(MEM_HEADS):
            kb_s[:, h * MEM_HEAD_DIM:(h + 1) * MEM_HEAD_DIM] = mk_ref[:, h, :].astype(BF16)
            vb_s[:, h * MEM_HEAD_DIM:(h + 1) * MEM_HEAD_DIM] = mv_ref[:, h, :].astype(BF16)
        if zero_state:
            carry_s[...] = jnp.zeros(carry_s.shape, F32)
        else:
            carry_s[...] = state_in[0][0]

    x = x_ref[0]
    hq = _rms(x, p_ref[FP_NORM_ATTN:FP_NORM_ATTN + 1, 0:D_MODEL]).astype(BF16)
    q = jnp.dot(hq, wq_ref[...], preferred_element_type=F32)
    head_cols = [slice(h * MEM_HEAD_DIM, (h + 1) * MEM_HEAD_DIM) for h in range(MEM_HEADS)]
    scores = [_mm_nt(q[:, sl], kb_s[:, sl]) * (MEM_HEAD_DIM ** -0.5) for sl in head_cols]
    expd = [jnp.exp(s - jnp.max(s, axis=-1, keepdims=True)) for s in scores]
    probs = [e / jnp.sum(e, axis=-1, keepdims=True) for e in expd]
    heads = [_mm(p, vb_s[:, sl]) for p, sl in zip(probs, head_cols)]
    x1 = x + _mm(jnp.concatenate(heads, axis=1), wo_ref[...])

    hf = _rms(x1, p_ref[FP_NORM_FFN:FP_NORM_FFN + 1, 0:D_MODEL]).astype(BF16)
    for j in range(N_FFN_BLK):
        halves = []
        for off in (j * FFN_BLK, D_FF + j * FFN_BLK):
            u = jnp.dot(hf, wup_ref[:, off:off + FFN_BLK], preferred_element_type=F32)
            halves.append(_causal_conv(u, carry_s[:, off:off + FFN_BLK],
                                       p_ref[FP_CONV:FP_CONV + FFN_CONV, off:off + FFN_BLK])
                          + p_ref[FP_CONVB:FP_CONVB + 1, off:off + FFN_BLK])
            carry_s[:, off:off + FFN_BLK] = u[tile - SUBLANES:tile]
        act_s[:, j * FFN_BLK:(j + 1) * FFN_BLK] = (_silu(halves[0]) * halves[1]).astype(BF16)
    y = x1 + jnp.dot(act_s[...], wdown_ref[...], preferred_element_type=F32)
    if final:
        y = _rms(y, p_ref[FP_NORM_FINAL:FP_NORM_FINAL + 1, 0:D_MODEL])
    y_ref[0] = y

    @pl.when(t == nt - 1)
    def _store():
        fconv_o[0] = carry_s[SUBLANES - (FFN_CONV - 1):, :]


def _memkv_kernel(mem_ref, g_ref, wk_ref, wv_ref, k_ref, v_ref):
    mn = _rms(mem_ref[...], g_ref[...]).astype(BF16)
    k = jnp.dot(mn, wk_ref[...], preferred_element_type=F32)
    v = jnp.dot(mn, wv_ref[...], preferred_element_type=F32)
    for h in range(MEM_HEADS):
        k_ref[:, h, :] = k[:, h * MEM_HEAD_DIM:(h + 1) * MEM_HEAD_DIM]
        v_ref[:, h, :] = v[:, h * MEM_HEAD_DIM:(h + 1) * MEM_HEAD_DIM]


def _layer_resident(layer, shape):
    nd = len(shape)
    return pl.BlockSpec((None,) + tuple(shape[1:]), lambda b, t: (layer,) + (0,) * (nd - 1),
                        pipeline_mode=pl.Buffered(1))


def _layer_rows(layer, shape, n_rows):
    nd = len(shape)
    return pl.BlockSpec((None, n_rows) + tuple(shape[2:]), lambda g, t: (layer, g) + (0,) * (nd - 2))


def _rows_out(shape, n_rows):
    nd = len(shape)
    return pl.BlockSpec((n_rows,) + tuple(shape[1:]), lambda g, t: (g,) + (0,) * (nd - 1))


def _seq_tile(length):
    return 256 if length % 256 == 0 else CHUNK


def _mix_call(layer, x, states, wts):
    bsz, length, _ = x.shape
    tile = _seq_tile(length)
    x_spec = pl.BlockSpec((N_INST, tile, D_MODEL), lambda g, t: (g, t, 0))
    state_shapes = [(bsz, A_CONV - 1, 3 * D_A), (bsz, H_A, DK_A, DV_A), (bsz, B_CONV - 1, D_B), (bsz, 1, D_B),
                    (bsz, H_C, DK_C, DK_C)]
    state_in = [] if states is None else list(states)
    return pl.pallas_call(
        functools.partial(_mix_kernel, layer, states is None),
        grid=(bsz // N_INST, length // tile),
        in_specs=[x_spec] + [_layer_rows(layer, s.shape, N_INST) for s in state_in]
        + [_layer_resident(layer, w.shape) for w in wts],
        out_specs=[x_spec] + [_rows_out(shp, N_INST) for shp in state_shapes],
        out_shape=[jax.ShapeDtypeStruct(x.shape, F32)] + [jax.ShapeDtypeStruct(shp, F32) for shp in state_shapes],
        scratch_shapes=[
            pltpu.VMEM((N_INST, tile, PROJ_W), F32),
            pltpu.VMEM((N_INST, tile, D_MODEL), F32),
            pltpu.VMEM((N_INST, SUBLANES, 3 * D_A), F32),
            pltpu.VMEM((N_INST, DK_A, H_A * DV_A), F32),
            pltpu.VMEM((N_INST, SUBLANES, D_B), F32),
            pltpu.VMEM((N_INST, 1, D_B), F32),
            pltpu.VMEM((N_INST, D_C, D_C), F32),
        ],
        compiler_params=pltpu.CompilerParams(
            dimension_semantics=("arbitrary", "arbitrary"), vmem_limit_bytes=VMEM_LIMIT_BYTES),
        name="mix_block",
    )(x, *state_in, *wts)


def _ffn_call(layer, final, x, mem_k, mem_v, fconv, wts):
    bsz, length, _ = x.shape
    tile = FFN_TILE if length % FFN_TILE == 0 else _seq_tile(length)
    x_spec = pl.BlockSpec((1, tile, D_MODEL), lambda b, t: (b, t, 0))
    kv_spec = pl.BlockSpec((None, None) + tuple(mem_k.shape[2:]), lambda b, t: (layer, b, 0, 0, 0))
    state_in = [] if fconv is None else [fconv]
    fconv_shape = (bsz, FFN_CONV - 1, 2 * D_FF)
    return pl.pallas_call(
        functools.partial(_ffn_kernel, final, fconv is None),
        grid=(bsz, length // tile),
        in_specs=[x_spec, kv_spec, kv_spec] + [_layer_rows(layer, s.shape, 1) for s in state_in]
        + [_layer_resident(layer, w.shape) for w in wts],
        out_specs=[x_spec, _rows_out(fconv_shape, 1)],
        out_shape=[jax.ShapeDtypeStruct(x.shape, F32), jax.ShapeDtypeStruct(fconv_shape, F32)],
        scratch_shapes=[
            pltpu.VMEM((N_MEM, D_MODEL), BF16),
            pltpu.VMEM((N_MEM, D_MODEL), BF16),
            pltpu.VMEM((SUBLANES, 2 * D_FF), F32),
            pltpu.VMEM((tile, D_FF), BF16),
        ],
        compiler_params=pltpu.CompilerParams(
            dimension_semantics=("arbitrary", "arbitrary"), vmem_limit_bytes=VMEM_LIMIT_BYTES),
        name="attn_ffn_block",
    )(x, mem_k, mem_v, *state_in, *wts)


def _memkv_call(mem, norm_mem, wk, wv):
    depth = wk.shape[0]
    bsz, n_mem, _ = mem.shape
    w_spec = pl.BlockSpec((None, D_MODEL, D_MODEL), lambda l, b: (l, 0, 0))
    o_spec = pl.BlockSpec((None, None, n_mem, MEM_HEADS, MEM_HEAD_DIM), lambda l, b: (l, b, 0, 0, 0))
    shape = jax.ShapeDtypeStruct((depth, bsz, n_mem, MEM_HEADS, MEM_HEAD_DIM), F32)
    return pl.pallas_call(
        _memkv_kernel,
        grid=(depth, bsz),
        in_specs=[pl.BlockSpec((None, n_mem, D_MODEL), lambda l, b: (b, 0, 0)),
                  pl.BlockSpec((None, 1, D_MODEL), lambda l, b: (l, 0, 0)), w_spec, w_spec],
        out_specs=[o_spec, o_spec],
        out_shape=[shape, shape],
        compiler_params=pltpu.CompilerParams(
            dimension_semantics=("arbitrary", "arbitrary"), vmem_limit_bytes=VMEM_LIMIT_BYTES),
        name="memory_kv",
    )(mem, norm_mem.reshape(depth, 1, D_MODEL), wk, wv)


def _pack_rows(pieces, width):
    padded = [jnp.pad(p, ((0, 0), (0, 0), (0, width - p.shape[2]))) for p in pieces]
    rows = sum(p.shape[1] for p in pieces)
    fill = jnp.zeros((pieces[0].shape[0], -rows % SUBLANES, width), F32)
    return jnp.concatenate(padded + [fill], axis=1)


def kernel(x_prompt, x_sample, cache_mem_k, cache_mem_v, state_a_conv, state_a_S, state_b_conv, state_b_h, state_c_S, state_ffn_conv, mem_prompt, norm_mix, w_in, a_conv_w, a_A_log, a_dt_bias, a_norm, b_conv_w, b_conv_b, b_w_r, b_b_r, b_w_i, b_b_i, b_lambda, c_lb_logits, c_norm, w_out, norm_attn, norm_mem, w_q, w_k, w_v, w_o, norm_ffn, w_up, ffn_conv_w, ffn_conv_b, w_down, norm_final):
    depth = w_in.shape[0]
    assert depth == N_LAYERS and x_prompt.shape[0] % N_INST == 0 and x_sample.shape[0] % N_INST == 0

    def rows(v):
        return v[:, None, :]

    w_in_a = w_in[:, :, :OFF_BA].astype(BF16)
    w_in_ba = jnp.pad(w_in[:, :, OFF_BA:OFF_BA + 2 * H_A], ((0, 0), (0, 0), (0, LANES - 2 * H_A))).astype(BF16)
    w_in_c = w_in[:, :, OFF_BA + 2 * H_A:].astype(BF16)
    eye = jnp.eye(B_BLOCKS, dtype=F32)
    w_gate = jnp.concatenate(
        [jnp.einsum('lnij,nm->lnimj', wg, eye).reshape(depth, D_B, D_B) for wg in (b_w_r, b_w_i)], axis=2).astype(BF16)
    lane_pad = jnp.zeros((depth, 1, H_A), F32)
    mix_p = _pack_rows([
        a_conv_w, rows(norm_mix), jnp.concatenate([lane_pad, rows(a_A_log)], axis=2),
        jnp.concatenate([lane_pad, rows(a_dt_bias)], axis=2), rows(a_norm),
        b_conv_w, rows(b_conv_b), rows(b_b_r), rows(b_b_i), rows(b_lambda), rows(jnp.tile(c_norm, (1, H_C))),
        jnp.broadcast_to(c_lb_logits[None], (depth,) + c_lb_logits.shape)], 3 * D_A)
    ffn_p = _pack_rows([
        ffn_conv_w, rows(ffn_conv_b), rows(norm_attn), rows(norm_ffn),
        jnp.broadcast_to(norm_final[None, None, :], (depth, 1, D_MODEL))], 2 * D_FF)
    mix_w = [mix_p, w_in_a, w_in_ba, w_in_c, w_gate, w_out.astype(BF16)]
    ffn_w = [ffn_p, w_q.astype(BF16), w_o.astype(BF16), w_up.astype(BF16), w_down.astype(BF16)]

    p_mem_k, p_mem_v = _memkv_call(mem_prompt, norm_mem, w_k.astype(BF16), w_v.astype(BF16))

    def pad8(s):
        return jnp.pad(s, ((0, 0), (0, 0), (SUBLANES - s.shape[2], 0), (0, 0)))

    def run_group(x, mem_k, mem_v, states):
        if states is not None:
            a_conv, a_s, b_conv, b_h, c_s, f_conv = states
            mix_states = [pad8(a_conv), a_s, pad8(b_conv), b_h[:, :, None, :], c_s]
            f_conv = pad8(f_conv)
        else:
            mix_states = f_conv = None
        outs = []
        for l in range(depth):
            x, na_conv, na_s, nb_conv, nb_h, nc_s = _mix_call(l, x, mix_states, mix_w)
            x, nf_conv = _ffn_call(l, l == depth - 1, x, mem_k, mem_v, f_conv, ffn_w)
            outs.append((na_conv, na_s, nb_conv, nb_h[:, 0], nc_s, nf_conv))
        return x, [jnp.stack(t, axis=0) for t in zip(*outs)]

    y_prompt, p_states = run_group(x_prompt, p_mem_k, p_mem_v, None)
    y_sample, s_states = run_group(
        x_sample, cache_mem_k, cache_mem_v,
        (state_a_conv, state_a_S, state_b_conv, state_b_h, state_c_S, state_ffn_conv))
    return (y_prompt, y_sample, p_mem_k, p_mem_v, *p_states, *s_states)
```

```python
import functools
import math

import jax
import jax.numpy as jnp
from jax import lax
from jax.experimental import pallas as pl
from jax.experimental.pallas import tpu as pltpu

F32 = jnp.float32
BF16 = jnp.bfloat16

D_MODEL = 1024
CHUNK = 64
D_A = 512
DK_A = 128
DV_A = 128
H_A = 4
A_CONV = 4
D_B = 256
B_BLOCKS = 4
B_CONV = 4
RG_C = 8.0
D_C = 256
H_C = 4
DK_C = 64
N_MEM = 256
MEM_HEADS = 4
MEM_HEAD_DIM = 256
D_FF = 2816
FFN_CONV = 3
RMS_EPS = 1e-6

LANES = 128
SUBLANES = 8
VMEM_LIMIT_BYTES = 56 * 1024 * 1024

OFF_QKV = 0
OFF_Z = 1536
OFF_BA = 2048
OFF_BX = 2176
OFF_BG = 2432
OFF_CQ = 2688
OFF_CF = 2944
OFF_CI = 3200
OFF_CG = 3456
N_PROJ = 3712
OFF_CK = 3712
OFF_G = 3968
PROJ_W = 4096
FFN_BLK = 256
N_FFN_BLK = D_FF // FFN_BLK
FFN_TILE = 512
FFN_ROWS = 256

N_LAYERS = 2

MP_ACONV, MP_NORM, MP_ALOG, MP_DT, MP_ANORM = 0, 4, 5, 6, 7
MP_BCONV, MP_BCONVB, MP_BR, MP_BI, MP_LAMBDA, MP_CNORM, MP_CLB = 8, 12, 13, 14, 15, 16, 17
FP_CONV, FP_CONVB, FP_NORM_ATTN, FP_NORM_FFN, FP_NORM_FINAL = 0, 3, 4, 5, 6

N_INST = 2
INST_SKEW = 12

_NT = (((1,), (1,)), ((), ()))
_TN = (((0,), (0,)), ((), ()))


def _mm(a, b):
    return jnp.dot(a.astype(BF16), b.astype(BF16), preferred_element_type=F32)


def _mm_nt(a, b):
    return lax.dot_general(a.astype(BF16), b.astype(BF16), _NT, preferred_element_type=F32)


def _split3(x):
    hi = x.astype(BF16)
    r = x - hi.astype(F32)
    mid = r.astype(BF16)
    lo = (r - mid.astype(F32)).astype(BF16)
    return hi, mid, lo


def _mm_sel(x, sel, parts=3):
    return sum(jnp.dot(part, sel, preferred_element_type=F32) for part in _split3(x)[:parts])


def _cumsum_rows(x):
    c = x.shape[0]
    tri3 = ((_iota((c, 3 * c), 1) % c) <= _iota((c, 3 * c), 0)).astype(BF16)
    return jnp.dot(tri3, jnp.concatenate(_split3(x), axis=0), preferred_element_type=F32)


def _rms(x, g):
    ms = jnp.mean(x * x, axis=-1, keepdims=True)
    return x * lax.rsqrt(ms + RMS_EPS) * g


def _silu(x):
    return x * jax.nn.sigmoid(x)


def _iota(shape, axis):
    return lax.broadcasted_iota(jnp.int32, shape, axis)


def _shift_rows(x, carries, s):
    seg = x.shape[0] // len(carries)
    rolled = pltpu.roll(x, s, 0)
    pieces = []
    for r, carry8 in enumerate(carries):
        lo = r * seg
        pieces.append(jnp.where(_iota(carry8.shape, 0) < s, pltpu.roll(carry8, s, 0), rolled[lo:lo + SUBLANES]))
        if seg > SUBLANES:
            pieces.append(rolled[lo + SUBLANES:lo + seg])
    return pieces[0] if len(pieces) == 1 else jnp.concatenate(pieces, axis=0)


def _causal_conv(x, carries, w):
    if not isinstance(carries, (list, tuple)):
        carries = [carries]
    width = w.shape[0]
    y = x * w[width - 1:width, :]
    for s in range(1, width):
        y = y + _shift_rows(x, carries, s) * w[width - 1 - s:width - s, :]
    return y


def _roll_in_groups(x, s):
    n, width = x.shape
    return pltpu.roll(x.reshape(n // SUBLANES, SUBLANES, width), s, 1).reshape(n, width)


def _lin_scan(a, u, h0):
    n = a.shape[0]
    row8 = _iota(a.shape, 0) % SUBLANES
    s = 1
    while s < SUBLANES:
        keep = row8 >= s
        a_s = jnp.where(keep, _roll_in_groups(a, s), 1.0)
        u_s = jnp.where(keep, _roll_in_groups(u, s), 0.0)
        u = a * u_s + u
        a = a * a_s
        s *= 2
    groups = []
    carry = h0
    for r in range(0, n, SUBLANES):
        groups.append(a[r:r + SUBLANES] * carry + u[r:r + SUBLANES])
        carry = groups[-1][SUBLANES - 1:SUBLANES]
    return jnp.concatenate(groups, axis=0)


def _tile_rows_masked(x, mask):
    return jnp.concatenate([x] * (mask.shape[0] // x.shape[0]), axis=0) * mask


def _chunk_masks():
    c = CHUNK
    n = H_A * c
    rowi = _iota((c, n), 0)
    lane_j = _iota((c, n), 1) % c
    m = {
        "half": _iota((c, LANES), 1) < c,
        "incl": lane_j <= rowi,
        "strict": lane_j < rowi,
        "eye": jnp.where(lane_j == rowi, 1.0, 0.0),
        "bd_sq": jnp.where(_iota((n, n), 0) // c == _iota((n, n), 1) // c, 1.0, 0.0).astype(BF16),
        "bd_wide": jnp.where(_iota((n, H_A * DV_A), 0) // c == _iota((n, H_A * DV_A), 1) // DV_A,
                             1.0, 0.0).astype(BF16),
        "ones3": jnp.ones((c, 3 * c), BF16),
    }
    for s in (32, 16, 8):
        m["odd", s] = ((_iota((c, D_C), 0) // s) % 2) == 1
        m["pair", s] = (rowi // (2 * s)) == (lane_j // (2 * s))
    for o in range(SUBLANES):
        m["valid", o] = (_iota((c, D_C), 0) % SUBLANES) >= o
        m["diag", o] = lane_j == rowi - o
    return m


def _each(fn, *lists):
    return [fn(*args) for args in zip(*lists)]


def _interleave(*gens):
    results = [None] * len(gens)
    live = dict(enumerate(gens))
    while live:
        for i, g in list(live.items()):
            try:
                next(g)
            except StopIteration as done:
                results[i] = done.value
                del live[i]
        yield
    return results


def _interleave_skewed(gens, skew):
    live = list(gens)
    step = 0
    while live:
        for k, g in enumerate(gens):
            if g in live and step >= k * skew:
                try:
                    next(g)
                except StopIteration:
                    live.remove(g)
        step += 1


def _gdn_prep(qs, ks, vs, beta_blks, g_blks, m):
    c = CHUNK
    dot = functools.partial(jnp.dot, preferred_element_type=F32)
    gcum = _each(_cumsum_rows, g_blks)
    yield

    def lane_forms(blk, first):
        cols = [jnp.broadcast_to(blk[:, first + h:first + h + 1], (c, LANES)) for h in range(H_A)]
        wide = jnp.concatenate(cols, axis=1)
        sq = jnp.concatenate([jnp.where(m["half"], cols[0], cols[1]), jnp.where(m["half"], cols[2], cols[3])],
                             axis=1)
        return wide, sq

    g_forms = _each(lambda g: lane_forms(g, H_A), gcum)
    yield
    b_forms = _each(lambda b: lane_forms(b, 0), beta_blks)
    yield
    gr = _each(lambda gf: dot(m["ones3"], jnp.concatenate(_split3(gf[1] * m["eye"]), axis=0)), g_forms)
    yield
    gamma = _each(lambda gf, r: jnp.exp(jnp.where(m["incl"], gf[1] - r, -jnp.inf)), g_forms, gr)
    yield
    kb = _each(lambda k: k.astype(BF16), ks)
    yield
    qk_kk = _each(lambda q, k_b: lax.dot_general(jnp.concatenate([q.astype(BF16), k_b], axis=0),
                                                 _tile_rows_masked(k_b, m["bd_wide"]), _NT,
                                                 preferred_element_type=F32), qs, kb)
    yield
    qk = _each(lambda r, gm: (r[:c] * gm).astype(BF16), qk_kk, gamma)
    yield
    a = _each(lambda bf, r, gm: jnp.where(m["strict"], bf[1] * r[c:] * gm, 0.0), b_forms, qk_kk, gamma)
    yield
    x = _each(lambda a_: m["eye"] - a_, a)
    yield
    ab = _each(lambda a_: a_.astype(BF16), a)
    yield
    p = _each(lambda a_b: dot(a_b, _tile_rows_masked(a_b, m["bd_sq"])), ab)
    yield
    for it in range(5):
        pb = _each(lambda p_: p_.astype(BF16), p)
        yield
        p_bd = _each(lambda p_b: _tile_rows_masked(p_b, m["bd_sq"]), pb)
        yield
        if it < 4:
            both = _each(lambda x_, p_b, bd: dot(jnp.concatenate([x_.astype(BF16), p_b], axis=0), bd), x, pb, p_bd)
            x = _each(lambda x_, r: x_ + r[:c], x, both)
            p = _each(lambda r: r[c:], both)
        else:
            x = _each(lambda x_, bd: x_ + dot(x_.astype(BF16), bd), x, p_bd)
    xb = _each(lambda x_: x_.astype(BF16), x)
    yield
    eg = _each(lambda gf: jnp.exp(gf[0]), g_forms)
    yield
    u = _each(lambda x_b, v, bf: dot(x_b, _tile_rows_masked((v * bf[0]).astype(BF16), m["bd_wide"])),
              xb, vs, b_forms)
    yield
    w = _each(lambda x_b, k, bf, e: dot(x_b, _tile_rows_masked((k * (bf[0] * e)).astype(BF16), m["bd_wide"])),
              xb, ks, b_forms, eg)
    yield
    glast = _each(lambda g: jnp.concatenate(
        [jnp.broadcast_to(g[c - 1:c, H_A + h:H_A + h + 1], (1, DV_A)) for h in range(H_A)], axis=1), gcum)
    yield
    kd = _each(lambda k, gl, gf: (k * jnp.exp(gl - gf[0])).astype(BF16), ks, glast, g_forms)
    yield
    kd_stack = _each(lambda kd_: jnp.concatenate([kd_[:, h * DK_A:(h + 1) * DK_A] for h in range(H_A)], axis=0), kd)
    yield
    wq = _each(lambda w_, q, e: [
        jnp.concatenate([w_[:, h * DK_A:(h + 1) * DK_A], (q * e)[:, h * DK_A:(h + 1) * DK_A]], axis=0).astype(BF16)
        for h in range(H_A)], w, qs, eg)
    decay = _each(jnp.exp, glast)
    yield
    return list(zip(u, wq, qk, kd_stack, decay))


def _gdn_step(prep, s_ref, m):
    u, wq, qk, kd_stack, decay = prep
    c = CHUNK
    s = s_ref[...]
    sb = s.astype(BF16)
    v_new, o_state = [], []
    for h in range(H_A):
        hs = slice(h * DV_A, (h + 1) * DV_A)
        ws = jnp.dot(wq[h], sb[:, hs], preferred_element_type=F32)
        v_new.append(u[:, hs] - ws[:c])
        o_state.append(ws[c:])
    v_bd = _tile_rows_masked(jnp.concatenate(v_new, axis=1).astype(BF16), m["bd_wide"])
    o = jnp.concatenate(o_state, axis=1) + jnp.dot(qk, v_bd, preferred_element_type=F32)
    s_ref[...] = s * decay + lax.dot_general(kd_stack, v_bd, _TN, preferred_element_type=F32)
    return o


def _hgrn_prep(qs, ks, vs, lfs, m):
    c = CHUNK
    d = D_C
    dot = functools.partial(jnp.dot, preferred_element_type=F32)
    bc = _each(_cumsum_rows, lfs)
    yield
    att = [jnp.zeros((c, H_C * c), F32) for _ in qs]
    for s in (32, 16, 8):
        odd = m["odd", s]
        ref = _each(lambda b: jnp.concatenate(
            [jnp.broadcast_to(b[r:r + 1, :], (2 * s, d)) for r in range(s, c, 2 * s)], axis=0), bc)
        yield
        e = _each(lambda b, rf: jnp.exp(jnp.where(odd, b - rf, rf - b)), bc, ref)
        yield
        def keep_blocks(x, want_odd):
            zero = jnp.zeros((s, d), F32)
            return jnp.concatenate(
                [x[r:r + s] if ((r // s) % 2 == 1) == want_odd else zero for r in range(0, c, s)], axis=0)

        qt = _each(lambda q, e_: keep_blocks(q * e_, True).astype(BF16), qs, e)
        yield
        kt = _each(lambda k, e_: keep_blocks(k * e_, False).astype(BF16), ks, e)
        yield
        lvl = _each(lambda q_t, k_t: lax.dot_general(q_t, _tile_rows_masked(k_t, m["bd_sq"]), _NT,
                                                     preferred_element_type=F32), qt, kt)
        yield
        att = _each(lambda a_, l: a_ + jnp.where(m["pair", s], l, 0.0), att, lvl)
        yield

    def shifted_prods(q, k, b):
        out = []
        for o in range(SUBLANES):
            ko = k if o == 0 else _roll_in_groups(k, o)
            bo = b if o == 0 else _roll_in_groups(b, o)
            out.append((q * ko * jnp.exp(jnp.where(m["valid", o], b - bo, -jnp.inf))).astype(BF16))
        return jnp.concatenate(out, axis=0)

    prods = _each(shifted_prods, qs, ks, bc)
    yield
    dsum = _each(lambda pr: dot(pr, m["bd_sq"]), prods)
    yield
    for o in range(SUBLANES):
        att = _each(lambda a_, ds: a_ + jnp.where(m["diag", o], ds[o * c:(o + 1) * c], 0.0), att, dsum)
        yield
    vb = _each(lambda v: v.astype(BF16), vs)
    yield
    o_intra = _each(lambda a_, v_b: dot(a_.astype(BF16), _tile_rows_masked(v_b, m["bd_sq"])), att, vb)
    yield
    qdec = _each(lambda q, b: (q * jnp.exp(b)).astype(BF16), qs, bc)
    yield
    kdec = _each(lambda k, b: (k * jnp.exp(b[c - 1:c, :] - b)).astype(BF16), ks, bc)
    yield
    decay = _each(lambda b: jnp.exp(b[c - 1:c, :]), bc)
    yield
    return list(zip(o_intra, qdec, kdec, vb, decay))


def _hgrn_step(prep, st_ref, m):
    o_intra, qdec, kdec, vb, decay = prep
    st = st_ref[...]
    o = o_intra + lax.dot_general(qdec, st.astype(BF16), _NT, preferred_element_type=F32)
    upd = lax.dot_general(vb, kdec, _TN, preferred_element_type=F32)
    st_ref[...] = st * decay + upd * m["bd_sq"].astype(F32)
    return o


def _mix_tile(layer, tile, x_ref, y_ref, proj, cat, aconv_s, sa_s, bconv_s, bh_s, sc_s, w, masks):
    p_ref, win_a_ref, win_ba_ref, win_c_ref, wgate_ref, wout_ref = w
    d = D_C
    x = x_ref[...]
    h_in = _rms(x, p_ref[MP_NORM:MP_NORM + 1, 0:D_MODEL]).astype(BF16)
    yield

    def in_proj(lo, width):
        if lo < OFF_BA:
            w_cols = win_a_ref[:, lo:lo + width]
        elif lo == OFF_BA:
            w_cols = win_ba_ref[...]
        else:
            w_cols = win_c_ref[:, lo - OFF_BX:lo - OFF_BX + width]
        return jnp.dot(h_in, w_cols, preferred_element_type=F32)

    for j in range(3 * D_A // FFN_BLK):
        lo = j * FFN_BLK
        pre = in_proj(lo, FFN_BLK)
        yield
        act = _silu(_causal_conv(pre, aconv_s[:, lo:lo + FFN_BLK], p_ref[MP_ACONV:MP_ACONV + A_CONV, lo:lo + FFN_BLK]))
        aconv_s[:, lo:lo + FFN_BLK] = pre[tile - SUBLANES:tile]
        if lo < 2 * D_A:
            for hh in range(FFN_BLK // DK_A):
                blk = act[:, hh * DK_A:(hh + 1) * DK_A]
                nrm = blk * lax.rsqrt(jnp.sum(blk * blk, axis=-1, keepdims=True) + 1e-6)
                if lo < D_A:
                    nrm = nrm * (DK_A ** -0.5)
                proj[:, lo + hh * DK_A:lo + (hh + 1) * DK_A] = nrm
        else:
            proj[:, lo:lo + FFN_BLK] = act
        yield
    ba = in_proj(OFF_BA, LANES)
    proj[:, OFF_G:OFF_G + LANES] = (-jnp.exp(p_ref[MP_ALOG:MP_ALOG + 1, 0:LANES])) * jax.nn.softplus(
        ba + p_ref[MP_DT:MP_DT + 1, 0:LANES])
    proj[:, OFF_BA:OFF_BA + LANES] = jax.nn.sigmoid(ba)
    yield

    bpre = in_proj(OFF_BX, D_B)
    yield
    xb = _causal_conv(bpre, bconv_s[...], p_ref[MP_BCONV:MP_BCONV + B_CONV, 0:D_B]) + p_ref[MP_BCONVB:MP_BCONVB + 1, 0:D_B]
    bconv_s[...] = bpre[tile - SUBLANES:tile]
    gates = _mm(xb, wgate_ref[...])
    r_gate = jax.nn.sigmoid(gates[:, :D_B] + p_ref[MP_BR:MP_BR + 1, 0:D_B])
    i_gate = jax.nn.sigmoid(gates[:, D_B:] + p_ref[MP_BI:MP_BI + 1, 0:D_B])
    yield
    log_a = (-RG_C) * r_gate * jax.nn.softplus(-p_ref[MP_LAMBDA:MP_LAMBDA + 1, 0:D_B])
    a_gate = jnp.exp(log_a)
    one_m_a2 = -jnp.tanh(log_a) * (a_gate * a_gate + 1.0)
    mult = jnp.where(one_m_a2 > 0.0, one_m_a2 * lax.rsqrt(one_m_a2), 0.0)
    yield
    for lo in (OFF_Z, OFF_Z + FFN_BLK, OFF_CI, OFF_CG):
        proj[:, lo:lo + FFN_BLK] = in_proj(lo, FFN_BLK)
        yield
    hb = _lin_scan(a_gate, mult * i_gate * xb, bh_s[...])
    bh_s[...] = hb[tile - 1:tile]
    yield
    cat[:, D_A:D_A + D_B] = hb * jax.nn.gelu(in_proj(OFF_BG, D_B))
    yield

    lg = p_ref[MP_CLB:MP_CLB + N_LAYERS, 0:d]
    ex = jnp.exp(lg - jnp.max(lg, axis=0, keepdims=True))
    sm = ex / jnp.sum(ex, axis=0, keepdims=True)
    lb = jnp.sum(sm[0:layer + 1], axis=0, keepdims=True) - sm[0:1]
    forget = lb + (1.0 - lb) * jax.nn.sigmoid(in_proj(OFF_CF, d))
    proj[:, OFF_CK:OFF_CK + d] = 1.0 - forget
    proj[:, OFF_CF:OFF_CF + d] = jnp.log(forget)
    yield
    proj[:, OFF_CQ:OFF_CQ + d] = _silu(in_proj(OFF_CQ, d))
    yield

    chunks = [slice(ci * CHUNK, (ci + 1) * CHUNK) for ci in range(tile // CHUNK)]

    def cols(lo, width):
        return [proj[rows, lo:lo + width] for rows in chunks]

    gdn = _gdn_prep(cols(0, D_A), cols(D_A, D_A), cols(2 * D_A, D_A), cols(OFF_BA, LANES), cols(OFF_G, LANES), masks)
    hgrn = _hgrn_prep(cols(OFF_CQ, d), cols(OFF_CK, d), cols(OFF_CI, d), cols(OFF_CF, d), masks)
    gdn_preps, hgrn_preps = yield from _interleave(gdn, hgrn)
    for rows, gdn_prep, hgrn_prep in zip(chunks, gdn_preps, hgrn_preps):
        cat[rows, 0:D_A] = _gdn_step(gdn_prep, sa_s, masks)
        yield
        cat[rows, D_A + D_B:D_MODEL] = _hgrn_step(hgrn_prep, sc_s, masks)
        yield

    for h in range(H_A):
        blk = cat[:, h * DV_A:(h + 1) * DV_A]
        ms = jnp.mean(blk * blk, axis=-1, keepdims=True)
        z = proj[:, OFF_Z + h * DV_A:OFF_Z + (h + 1) * DV_A]
        cat[:, h * DV_A:(h + 1) * DV_A] = (blk * lax.rsqrt(ms + RMS_EPS) * p_ref[MP_ANORM:MP_ANORM + 1, 0:DV_A]
                                           * _silu(z))
        yield
    oc = cat[:, D_A + D_B:D_MODEL]
    ms_c = _mm_sel(oc * oc, masks["bd_sq"], parts=2) * (1.0 / DK_C)
    cat[:, D_A + D_B:D_MODEL] = (oc * lax.rsqrt(ms_c + RMS_EPS) * p_ref[MP_CNORM:MP_CNORM + 1, 0:d]
                                 * _silu(proj[:, OFF_CG:OFF_CG + d]))
    yield
    y_ref[...] = x + _mm(cat[...], wout_ref[...])


def _mix_kernel(layer, zero_state, *refs):
    n_state_in = 0 if zero_state else 5
    x_ref = refs[0]
    state_in = refs[1:1 + n_state_in]
    w = refs[1 + n_state_in:7 + n_state_in]
    y_ref, aconv_o, as_o, bconv_o, bh_o, cs_o, proj, cat, aconv_s, sa_s, bconv_s, bh_s, sc_s = refs[7 + n_state_in:]
    t = pl.program_id(1)
    nt = pl.num_programs(1)
    n_inst, tile = x_ref.shape[0], x_ref.shape[1]
    d = D_C
    masks = _chunk_masks()

    @pl.when(t == 0)
    def _load_state():
        if zero_state:
            for ref in (aconv_s, sa_s, bconv_s, bh_s, sc_s):
                ref[...] = jnp.zeros(ref.shape, F32)
            return
        aconv_ref, as_ref, bconv_ref, bh_ref, cs_ref = state_in
        for i in range(n_inst):
            aconv_s[i] = aconv_ref[i]
            bconv_s[i] = bconv_ref[i]
            bh_s[i] = bh_ref[i]
            for h in range(H_A):
                sa_s[i, :, h * DV_A:(h + 1) * DV_A] = as_ref[i, h]
            rows = jnp.concatenate([cs_ref[i, h] for h in range(H_C)], axis=0)
            tile_sel = jnp.where(_iota((DK_C, d), 1) % DK_C == _iota((DK_C, d), 0), 1.0, 0.0).astype(BF16)
            s_bd = _mm_sel(rows, tile_sel) * masks["bd_sq"].astype(F32)
            sc_s[i] = s_bd.T

    _interleave_skewed(
        [_mix_tile(layer, tile, x_ref.at[i], y_ref.at[i], proj.at[i], cat.at[i], aconv_s.at[i], sa_s.at[i],
                   bconv_s.at[i], bh_s.at[i], sc_s.at[i], w, masks) for i in range(n_inst)], INST_SKEW)

    @pl.when(t == nt - 1)
    def _store_state():
        for i in range(n_inst):
            aconv_o[i] = aconv_s[i][SUBLANES - (A_CONV - 1):]
            bconv_o[i] = bconv_s[i][SUBLANES - (B_CONV - 1):]
            bh_o[i] = bh_s[i]
            for h in range(H_A):
                as_o[i, h] = sa_s[i, :, h * DV_A:(h + 1) * DV_A]
            s_bd = sc_s[i].T
            for h in range(H_C):
                sel = jnp.where(_iota((d, DK_C), 0) == _iota((d, DK_C), 1) + h * DK_C, 1.0, 0.0).astype(BF16)
                cs_o[i, h] = _mm_sel(s_bd[h * DK_C:(h + 1) * DK_C, :], sel)


def _ffn_kernel(final, zero_state, kv_packed, *refs):
    n_state_in = 0 if zero_state else 1
    x_ref, mk_ref, mv_ref = refs[:3]
    state_in = refs[3:3 + n_state_in]
    p_ref, wq_ref, wo_ref, wup_ref, wdown_ref, y_ref, fconv_o = refs[3 + n_state_in:10 + n_state_in]
    scratch = refs[10 + n_state_in:]
    kb_s, vb_s = (mk_ref, mv_ref) if kv_packed else scratch[:2]
    carry_s, act_s = scratch[-2:]
    t = pl.program_id(1)
    nt = pl.num_programs(1)
    n_rows, tile = x_ref.shape[0], x_ref.shape[1]

    @pl.when(t == 0)
    def _load():
        if not kv_packed:
            for r in range(n_rows):
                for h in range(MEM_HEADS):
                    kb_s[r, :, h * MEM_HEAD_DIM:(h + 1) * MEM_HEAD_DIM] = mk_ref[r, :, h, :].astype(BF16)
                    vb_s[r, :, h * MEM_HEAD_DIM:(h + 1) * MEM_HEAD_DIM] = mv_ref[r, :, h, :].astype(BF16)
        if zero_state:
            carry_s[...] = jnp.zeros(carry_s.shape, F32)
        else:
            carry_s[...] = state_in[0][...]

    x = x_ref[...].reshape(n_rows * tile, D_MODEL)
    hq = _rms(x, p_ref[FP_NORM_ATTN:FP_NORM_ATTN + 1, 0:D_MODEL]).astype(BF16)
    q = jnp.dot(hq, wq_ref[...], preferred_element_type=F32)
    units = [(r, slice(r * tile, (r + 1) * tile), slice(h * MEM_HEAD_DIM, (h + 1) * MEM_HEAD_DIM))
             for r in range(n_rows) for h in range(MEM_HEADS)]
    scores = [_mm_nt(q[rs, cs], kb_s[r, :, cs]) * (MEM_HEAD_DIM ** -0.5) for r, rs, cs in units]
    expd = [jnp.exp(s - jnp.max(s, axis=-1, keepdims=True)) for s in scores]
    probs = [e / jnp.sum(e, axis=-1, keepdims=True) for e in expd]
    outs = [_mm(p, vb_s[r, :, cs]) for p, (r, rs, cs) in zip(probs, units)]
    attn = jnp.concatenate([jnp.concatenate(outs[r * MEM_HEADS:(r + 1) * MEM_HEADS], axis=1) for r in range(n_rows)],
                           axis=0)
    x1 = x + _mm(attn, wo_ref[...])

    hf = _rms(x1, p_ref[FP_NORM_FFN:FP_NORM_FFN + 1, 0:D_MODEL]).astype(BF16)
    for j in range(N_FFN_BLK):
        halves = []
        for off in (j * FFN_BLK, D_FF + j * FFN_BLK):
            u = jnp.dot(hf, wup_ref[:, off:off + FFN_BLK], preferred_element_type=F32)
            halves.append(_causal_conv(u, [carry_s[r, :, off:off + FFN_BLK] for r in range(n_rows)],
                                       p_ref[FP_CONV:FP_CONV + FFN_CONV, off:off + FFN_BLK])
                          + p_ref[FP_CONVB:FP_CONVB + 1, off:off + FFN_BLK])
            for r in range(n_rows):
                carry_s[r, :, off:off + FFN_BLK] = u[(r + 1) * tile - SUBLANES:(r + 1) * tile]
        act_s[:, j * FFN_BLK:(j + 1) * FFN_BLK] = (_silu(halves[0]) * halves[1]).astype(BF16)
    y = x1 + jnp.dot(act_s[...], wdown_ref[...], preferred_element_type=F32)
    if final:
        y = _rms(y, p_ref[FP_NORM_FINAL:FP_NORM_FINAL + 1, 0:D_MODEL])
    y_ref[...] = y.reshape(n_rows, tile, D_MODEL)

    @pl.when(t == nt - 1)
    def _store():
        for r in range(n_rows):
            fconv_o[r] = carry_s[r, SUBLANES - (FFN_CONV - 1):, :]


def _memkv_kernel(mem_ref, g_ref, wk_ref, wv_ref, k_ref, v_ref, kb_ref, vb_ref):
    mn = _rms(mem_ref[...], g_ref[...]).astype(BF16)
    k = jnp.dot(mn, wk_ref[...], preferred_element_type=F32)
    v = jnp.dot(mn, wv_ref[...], preferred_element_type=F32)
    kb_ref[...] = k.astype(BF16)
    vb_ref[...] = v.astype(BF16)
    for h in range(MEM_HEADS):
        k_ref[:, h, :] = k[:, h * MEM_HEAD_DIM:(h + 1) * MEM_HEAD_DIM]
        v_ref[:, h, :] = v[:, h * MEM_HEAD_DIM:(h + 1) * MEM_HEAD_DIM]


def _layer_resident(layer, shape):
    nd = len(shape)
    return pl.BlockSpec((None,) + tuple(shape[1:]), lambda b, t: (layer,) + (0,) * (nd - 1),
                        pipeline_mode=pl.Buffered(1))


def _layer_rows(layer, shape, n_rows):
    nd = len(shape)
    return pl.BlockSpec((None, n_rows) + tuple(shape[2:]), lambda g, t: (layer, g) + (0,) * (nd - 2))


def _rows_out(shape, n_rows):
    nd = len(shape)
    return pl.BlockSpec((n_rows,) + tuple(shape[1:]), lambda g, t: (g,) + (0,) * (nd - 1))


def _seq_tile(length):
    return 256 if length % 256 == 0 else CHUNK


def _mix_call(layer, x, states, wts):
    bsz, length, _ = x.shape
    tile = _seq_tile(length)
    x_spec = pl.BlockSpec((N_INST, tile, D_MODEL), lambda g, t: (g, t, 0))
    state_shapes = [(bsz, A_CONV - 1, 3 * D_A), (bsz, H_A, DK_A, DV_A), (bsz, B_CONV - 1, D_B), (bsz, 1, D_B),
                    (bsz, H_C, DK_C, DK_C)]
    state_in = [] if states is None else list(states)
    return pl.pallas_call(
        functools.partial(_mix_kernel, layer, states is None),
        grid=(bsz // N_INST, length // tile),
        in_specs=[x_spec] + [_layer_rows(layer, s.shape, N_INST) for s in state_in]
        + [_layer_resident(layer, w.shape) for w in wts],
        out_specs=[x_spec] + [_rows_out(shp, N_INST) for shp in state_shapes],
        out_shape=[jax.ShapeDtypeStruct(x.shape, F32)] + [jax.ShapeDtypeStruct(shp, F32) for shp in state_shapes],
        scratch_shapes=[
            pltpu.VMEM((N_INST, tile, PROJ_W), F32),
            pltpu.VMEM((N_INST, tile, D_MODEL), F32),
            pltpu.VMEM((N_INST, SUBLANES, 3 * D_A), F32),
            pltpu.VMEM((N_INST, DK_A, H_A * DV_A), F32),
            pltpu.VMEM((N_INST, SUBLANES, D_B), F32),
            pltpu.VMEM((N_INST, 1, D_B), F32),
            pltpu.VMEM((N_INST, D_C, D_C), F32),
        ],
        compiler_params=pltpu.CompilerParams(
            dimension_semantics=("arbitrary", "arbitrary"), vmem_limit_bytes=VMEM_LIMIT_BYTES),
        name="mix_block",
    )(x, *state_in, *wts)


def _ffn_call(layer, final, x, mem_k, mem_v, fconv, wts):
    bsz, length, _ = x.shape
    tile = FFN_TILE if length % FFN_TILE == 0 else _seq_tile(length)
    n_rows = math.gcd(bsz, max(1, FFN_ROWS // tile))
    x_spec = pl.BlockSpec((n_rows, tile, D_MODEL), lambda g, t: (g, t, 0))
    kv_packed = mem_k.dtype == BF16
    kv_buffers = {} if n_rows == 1 else {"pipeline_mode": pl.Buffered(1)}
    kv_spec = pl.BlockSpec((None, n_rows) + tuple(mem_k.shape[2:]),
                           lambda g, t: (layer, g) + (0,) * (mem_k.ndim - 2), **kv_buffers)
    kv_scratch = [] if kv_packed else [pltpu.VMEM((n_rows, N_MEM, D_MODEL), BF16)] * 2
    state_in = [] if fconv is None else [fconv]
    fconv_shape = (bsz, FFN_CONV - 1, 2 * D_FF)
    return pl.pallas_call(
        functools.partial(_ffn_kernel, final, fconv is None, kv_packed),
        grid=(bsz // n_rows, length // tile),
        in_specs=[x_spec, kv_spec, kv_spec] + [_layer_rows(layer, s.shape, n_rows) for s in state_in]
        + [_layer_resident(layer, w.shape) for w in wts],
        out_specs=[x_spec, _rows_out(fconv_shape, n_rows)],
        out_shape=[jax.ShapeDtypeStruct(x.shape, F32), jax.ShapeDtypeStruct(fconv_shape, F32)],
        scratch_shapes=kv_scratch + [
            pltpu.VMEM((n_rows, SUBLANES, 2 * D_FF), F32),
            pltpu.VMEM((n_rows * tile, D_FF), BF16),
        ],
        compiler_params=pltpu.CompilerParams(
            dimension_semantics=("arbitrary", "arbitrary"), vmem_limit_bytes=VMEM_LIMIT_BYTES),
        name="attn_ffn_block",
    )(x, mem_k, mem_v, *state_in, *wts)


def _memkv_call(mem, norm_mem, wk, wv):
    depth = wk.shape[0]
    bsz, n_mem, _ = mem.shape
    w_spec = pl.BlockSpec((None, D_MODEL, D_MODEL), lambda l, b: (l, 0, 0))
    o_spec = pl.BlockSpec((None, None, n_mem, MEM_HEADS, MEM_HEAD_DIM), lambda l, b: (l, b, 0, 0, 0))
    shape = jax.ShapeDtypeStruct((depth, bsz, n_mem, MEM_HEADS, MEM_HEAD_DIM), F32)
    b_spec = pl.BlockSpec((None, None, n_mem, D_MODEL), lambda l, b: (l, b, 0, 0))
    b_shape = jax.ShapeDtypeStruct((depth, bsz, n_mem, D_MODEL), BF16)
    return pl.pallas_call(
        _memkv_kernel,
        grid=(depth, bsz),
        in_specs=[pl.BlockSpec((None, n_mem, D_MODEL), lambda l, b: (b, 0, 0)),
                  pl.BlockSpec((None, 1, D_MODEL), lambda l, b: (l, 0, 0)), w_spec, w_spec],
        out_specs=[o_spec, o_spec, b_spec, b_spec],
        out_shape=[shape, shape, b_shape, b_shape],
        compiler_params=pltpu.CompilerParams(
            dimension_semantics=("arbitrary", "arbitrary"), vmem_limit_bytes=VMEM_LIMIT_BYTES),
        name="memory_kv",
    )(mem, norm_mem.reshape(depth, 1, D_MODEL), wk, wv)


def _pack_rows(pieces, width):
    padded = [jnp.pad(p, ((0, 0), (0, 0), (0, width - p.shape[2]))) for p in pieces]
    rows = sum(p.shape[1] for p in pieces)
    fill = jnp.zeros((pieces[0].shape[0], -rows % SUBLANES, width), F32)
    return jnp.concatenate(padded + [fill], axis=1)


def kernel(x_prompt, x_sample, cache_mem_k, cache_mem_v, state_a_conv, state_a_S, state_b_conv, state_b_h, state_c_S, state_ffn_conv, mem_prompt, norm_mix, w_in, a_conv_w, a_A_log, a_dt_bias, a_norm, b_conv_w, b_conv_b, b_w_r, b_b_r, b_w_i, b_b_i, b_lambda, c_lb_logits, c_norm, w_out, norm_attn, norm_mem, w_q, w_k, w_v, w_o, norm_ffn, w_up, ffn_conv_w, ffn_conv_b, w_down, norm_final):
    depth = w_in.shape[0]
    assert depth == N_LAYERS and x_prompt.shape[0] % N_INST == 0 and x_sample.shape[0] % N_INST == 0

    def rows(v):
        return v[:, None, :]

    w_in_a = w_in[:, :, :OFF_BA].astype(BF16)
    w_in_ba = jnp.pad(w_in[:, :, OFF_BA:OFF_BA + 2 * H_A], ((0, 0), (0, 0), (0, LANES - 2 * H_A))).astype(BF16)
    w_in_c = w_in[:, :, OFF_BA + 2 * H_A:].astype(BF16)
    eye = jnp.eye(B_BLOCKS, dtype=F32)
    w_gate = jnp.concatenate(
        [jnp.einsum('lnij,nm->lnimj', wg, eye).reshape(depth, D_B, D_B) for wg in (b_w_r, b_w_i)], axis=2).astype(BF16)
    lane_pad = jnp.zeros((depth, 1, H_A), F32)
    mix_p = _pack_rows([
        a_conv_w, rows(norm_mix), jnp.concatenate([lane_pad, rows(a_A_log)], axis=2),
        jnp.concatenate([lane_pad, rows(a_dt_bias)], axis=2), rows(a_norm),
        b_conv_w, rows(b_conv_b), rows(b_b_r), rows(b_b_i), rows(b_lambda), rows(jnp.tile(c_norm, (1, H_C))),
        jnp.broadcast_to(c_lb_logits[None], (depth,) + c_lb_logits.shape)], 3 * D_A)
    ffn_p = _pack_rows([
        ffn_conv_w, rows(ffn_conv_b), rows(norm_attn), rows(norm_ffn),
        jnp.broadcast_to(norm_final[None, None, :], (depth, 1, D_MODEL))], 2 * D_FF)
    mix_w = [mix_p, w_in_a, w_in_ba, w_in_c, w_gate, w_out.astype(BF16)]
    ffn_w = [ffn_p, w_q.astype(BF16), w_o.astype(BF16), w_up.astype(BF16), w_down.astype(BF16)]

    p_mem_k, p_mem_v, p_mem_kb, p_mem_vb = _memkv_call(mem_prompt, norm_mem, w_k.astype(BF16), w_v.astype(BF16))

    def pad8(s):
        return jnp.pad(s, ((0, 0), (0, 0), (SUBLANES - s.shape[2], 0), (0, 0)))

    def run_group(x, mem_k, mem_v, states):
        if states is not None:
            a_conv, a_s, b_conv, b_h, c_s, f_conv = states
            mix_states = [pad8(a_conv), a_s, pad8(b_conv), b_h[:, :, None, :], c_s]
            f_conv = pad8(f_conv)
        else:
            mix_states = f_conv = None
        outs = []
        for l in range(depth):
            x, na_conv, na_s, nb_conv, nb_h, nc_s = _mix_call(l, x, mix_states, mix_w)
            x, nf_conv = _ffn_call(l, l == depth - 1, x, mem_k, mem_v, f_conv, ffn_w)
            outs.append((na_conv, na_s, nb_conv, nb_h[:, 0], nc_s, nf_conv))
        return x, [jnp.stack(t, axis=0) for t in zip(*outs)]

    y_prompt, p_states = run_group(x_prompt, p_mem_kb, p_mem_vb, None)
    y_sample, s_states = run_group(
        x_sample, cache_mem_k, cache_mem_v,
        (state_a_conv, state_a_S, state_b_conv, state_b_h, state_c_S, state_ffn_conv))
    return (y_prompt, y_sample, p_mem_k, p_mem_v, *p_states, *s_states)
```

```python
import functools
import math

import jax
import jax.numpy as jnp
from jax import lax
from jax.experimental import pallas as pl
from jax.experimental.pallas import tpu as pltpu

F32 = jnp.float32
BF16 = jnp.bfloat16

D_MODEL = 1024
CHUNK = 64
D_A = 512
DK_A = 128
DV_A = 128
H_A = 4
A_CONV = 4
D_B = 256
B_BLOCKS = 4
B_CONV = 4
RG_C = 8.0
D_C = 256
H_C = 4
DK_C = 64
N_MEM = 256
MEM_HEADS = 4
MEM_HEAD_DIM = 256
D_FF = 2816
FFN_CONV = 3
RMS_EPS = 1e-6

LANES = 128
SUBLANES = 8
VMEM_LIMIT_BYTES = 56 * 1024 * 1024

OFF_QKV = 0
OFF_Z = 1536
OFF_BA = 2048
OFF_BX = 2176
OFF_BG = 2432
OFF_CQ = 2688
OFF_CF = 2944
OFF_CI = 3200
OFF_CG = 3456
N_PROJ = 3712
OFF_CK = 3712
OFF_G = 3968
PROJ_W = 4096
FFN_BLK = 256
N_FFN_BLK = D_FF // FFN_BLK
FFN_TILE = 512
FFN_ROWS = 256

N_LAYERS = 2

MP_ACONV, MP_NORM, MP_ALOG, MP_DT, MP_ANORM = 0, 4, 5, 6, 7
MP_BCONV, MP_BCONVB, MP_BR, MP_BI, MP_LAMBDA, MP_CNORM, MP_CLB = 8, 12, 13, 14, 15, 16, 17
FP_CONV, FP_CONVB, FP_NORM_ATTN, FP_NORM_FFN, FP_NORM_FINAL = 0, 3, 4, 5, 6

N_INST = 4
INST_SKEW = 12

_NT = (((1,), (1,)), ((), ()))
_TN = (((0,), (0,)), ((), ()))


def _mm(a, b):
    return jnp.dot(a.astype(BF16), b.astype(BF16), preferred_element_type=F32)


def _mm_nt(a, b):
    return lax.dot_general(a.astype(BF16), b.astype(BF16), _NT, preferred_element_type=F32)


def _split3(x):
    hi = x.astype(BF16)
    r = x - hi.astype(F32)
    mid = r.astype(BF16)
    lo = (r - mid.astype(F32)).astype(BF16)
    return hi, mid, lo


def _mm_sel(x, sel, parts=3):
    return sum(jnp.dot(part, sel, preferred_element_type=F32) for part in _split3(x)[:parts])


def _cumsum_rows(x):
    c = x.shape[0]
    tri3 = ((_iota((c, 3 * c), 1) % c) <= _iota((c, 3 * c), 0)).astype(BF16)
    return jnp.dot(tri3, jnp.concatenate(_split3(x), axis=0), preferred_element_type=F32)


def _rms(x, g):
    ms = jnp.mean(x * x, axis=-1, keepdims=True)
    return x * lax.rsqrt(ms + RMS_EPS) * g


def _silu(x):
    return x * jax.nn.sigmoid(x)


def _iota(shape, axis):
    return lax.broadcasted_iota(jnp.int32, shape, axis)


def _shift_rows(x, carries, s):
    seg = x.shape[0] // len(carries)
    rolled = pltpu.roll(x, s, 0)
    pieces = []
    for r, carry8 in enumerate(carries):
        lo = r * seg
        pieces.append(jnp.where(_iota(carry8.shape, 0) < s, pltpu.roll(carry8, s, 0), rolled[lo:lo + SUBLANES]))
        if seg > SUBLANES:
            pieces.append(rolled[lo + SUBLANES:lo + seg])
    return pieces[0] if len(pieces) == 1 else jnp.concatenate(pieces, axis=0)


def _causal_conv(x, carries, w):
    if not isinstance(carries, (list, tuple)):
        carries = [carries]
    width = w.shape[0]
    y = x * w[width - 1:width, :]
    for s in range(1, width):
        y = y + _shift_rows(x, carries, s) * w[width - 1 - s:width - s, :]
    return y


def _roll_in_groups(x, s):
    n, width = x.shape
    return pltpu.roll(x.reshape(n // SUBLANES, SUBLANES, width), s, 1).reshape(n, width)


def _lin_scan(a, u, h0):
    n = a.shape[0]
    row8 = _iota(a.shape, 0) % SUBLANES
    s = 1
    while s < SUBLANES:
        keep = row8 >= s
        a_s = jnp.where(keep, _roll_in_groups(a, s), 1.0)
        u_s = jnp.where(keep, _roll_in_groups(u, s), 0.0)
        u = a * u_s + u
        a = a * a_s
        s *= 2
    groups = []
    carry = h0
    for r in range(0, n, SUBLANES):
        groups.append(a[r:r + SUBLANES] * carry + u[r:r + SUBLANES])
        carry = groups[-1][SUBLANES - 1:SUBLANES]
    return jnp.concatenate(groups, axis=0)


def _tile_rows_masked(x, mask):
    return jnp.concatenate([x] * (mask.shape[0] // x.shape[0]), axis=0) * mask


def _chunk_masks():
    c = CHUNK
    n = H_A * c
    rowi = _iota((c, n), 0)
    lane_j = _iota((c, n), 1) % c
    m = {
        "half": _iota((c, LANES), 1) < c,
        "incl": lane_j <= rowi,
        "strict": lane_j < rowi,
        "eye": jnp.where(lane_j == rowi, 1.0, 0.0),
        "bd_sq": jnp.where(_iota((n, n), 0) // c == _iota((n, n), 1) // c, 1.0, 0.0).astype(BF16),
        "bd_wide": jnp.where(_iota((n, H_A * DV_A), 0) // c == _iota((n, H_A * DV_A), 1) // DV_A,
                             1.0, 0.0).astype(BF16),
        "ones3": jnp.ones((c, 3 * c), BF16),
    }
    for s in (32, 16, 8):
        m["odd", s] = ((_iota((c, D_C), 0) // s) % 2) == 1
        m["pair", s] = (rowi // (2 * s)) == (lane_j // (2 * s))
    for o in range(SUBLANES):
        m["valid", o] = (_iota((c, D_C), 0) % SUBLANES) >= o
        m["diag", o] = lane_j == rowi - o
    return m


def _each(fn, *lists):
    return [fn(*args) for args in zip(*lists)]


def _interleave(*gens):
    results = [None] * len(gens)
    live = dict(enumerate(gens))
    while live:
        for i, g in list(live.items()):
            try:
                next(g)
            except StopIteration as done:
                results[i] = done.value
                del live[i]
        yield
    return results


def _interleave_skewed(gens, skew):
    live = list(gens)
    step = 0
    while live:
        for k, g in enumerate(gens):
            if g in live and step >= k * skew:
                try:
                    next(g)
                except StopIteration:
                    live.remove(g)
        step += 1


def _gdn_prep(qs, ks, vs, beta_blks, g_blks, m):
    c = CHUNK
    dot = functools.partial(jnp.dot, preferred_element_type=F32)
    gcum = _each(_cumsum_rows, g_blks)
    yield

    def lane_forms(blk, first):
        cols = [jnp.broadcast_to(blk[:, first + h:first + h + 1], (c, LANES)) for h in range(H_A)]
        wide = jnp.concatenate(cols, axis=1)
        sq = jnp.concatenate([jnp.where(m["half"], cols[0], cols[1]), jnp.where(m["half"], cols[2], cols[3])],
                             axis=1)
        return wide, sq

    g_forms = _each(lambda g: lane_forms(g, H_A), gcum)
    yield
    b_forms = _each(lambda b: lane_forms(b, 0), beta_blks)
    yield
    gr = _each(lambda gf: dot(m["ones3"], jnp.concatenate(_split3(gf[1] * m["eye"]), axis=0)), g_forms)
    yield
    gamma = _each(lambda gf, r: jnp.exp(jnp.where(m["incl"], gf[1] - r, -jnp.inf)), g_forms, gr)
    yield
    kb = _each(lambda k: k.astype(BF16), ks)
    yield
    qk_kk = _each(lambda q, k_b: lax.dot_general(jnp.concatenate([q.astype(BF16), k_b], axis=0),
                                                 _tile_rows_masked(k_b, m["bd_wide"]), _NT,
                                                 preferred_element_type=F32), qs, kb)
    yield
    qk = _each(lambda r, gm: (r[:c] * gm).astype(BF16), qk_kk, gamma)
    yield
    a = _each(lambda bf, r, gm: jnp.where(m["strict"], bf[1] * r[c:] * gm, 0.0), b_forms, qk_kk, gamma)
    yield
    x = _each(lambda a_: m["eye"] - a_, a)
    yield
    ab = _each(lambda a_: a_.astype(BF16), a)
    yield
    p = _each(lambda a_b: dot(a_b, _tile_rows_masked(a_b, m["bd_sq"])), ab)
    yield
    for it in range(5):
        pb = _each(lambda p_: p_.astype(BF16), p)
        yield
        p_bd = _each(lambda p_b: _tile_rows_masked(p_b, m["bd_sq"]), pb)
        yield
        if it < 4:
            both = _each(lambda x_, p_b, bd: dot(jnp.concatenate([x_.astype(BF16), p_b], axis=0), bd), x, pb, p_bd)
            x = _each(lambda x_, r: x_ + r[:c], x, both)
            p = _each(lambda r: r[c:], both)
        else:
            x = _each(lambda x_, bd: x_ + dot(x_.astype(BF16), bd), x, p_bd)
    xb = _each(lambda x_: x_.astype(BF16), x)
    yield
    eg = _each(lambda gf: jnp.exp(gf[0]), g_forms)
    yield
    u = _each(lambda x_b, v, bf: dot(x_b, _tile_rows_masked((v * bf[0]).astype(BF16), m["bd_wide"])),
              xb, vs, b_forms)
    yield
    w = _each(lambda x_b, k, bf, e: dot(x_b, _tile_rows_masked((k * (bf[0] * e)).astype(BF16), m["bd_wide"])),
              xb, ks, b_forms, eg)
    yield
    glast = _each(lambda g: jnp.concatenate(
        [jnp.broadcast_to(g[c - 1:c, H_A + h:H_A + h + 1], (1, DV_A)) for h in range(H_A)], axis=1), gcum)
    yield
    kd = _each(lambda k, gl, gf: (k * jnp.exp(gl - gf[0])).astype(BF16), ks, glast, g_forms)
    yield
    kd_stack = _each(lambda kd_: jnp.concatenate([kd_[:, h * DK_A:(h + 1) * DK_A] for h in range(H_A)], axis=0), kd)
    yield
    wq = _each(lambda w_, q, e: [
        jnp.concatenate([w_[:, h * DK_A:(h + 1) * DK_A], (q * e)[:, h * DK_A:(h + 1) * DK_A]], axis=0).astype(BF16)
        for h in range(H_A)], w, qs, eg)
    decay = _each(jnp.exp, glast)
    yield
    return list(zip(u, wq, qk, kd_stack, decay))


def _gdn_step(prep, s_ref, m):
    u, wq, qk, kd_stack, decay = prep
    c = CHUNK
    s = s_ref[...]
    sb = s.astype(BF16)
    v_new, o_state = [], []
    for h in range(H_A):
        hs = slice(h * DV_A, (h + 1) * DV_A)
        ws = jnp.dot(wq[h], sb[:, hs], preferred_element_type=F32)
        v_new.append(u[:, hs] - ws[:c])
        o_state.append(ws[c:])
    v_bd = _tile_rows_masked(jnp.concatenate(v_new, axis=1).astype(BF16), m["bd_wide"])
    o = jnp.concatenate(o_state, axis=1) + jnp.dot(qk, v_bd, preferred_element_type=F32)
    s_ref[...] = s * decay + lax.dot_general(kd_stack, v_bd, _TN, preferred_element_type=F32)
    return o


def _hgrn_prep(qs, ks, vs, lfs, m):
    c = CHUNK
    d = D_C
    dot = functools.partial(jnp.dot, preferred_element_type=F32)
    bc = _each(_cumsum_rows, lfs)
    yield
    att = [jnp.zeros((c, H_C * c), F32) for _ in qs]
    for s in (32, 16, 8):
        odd = m["odd", s]
        ref = _each(lambda b: jnp.concatenate(
            [jnp.broadcast_to(b[r:r + 1, :], (2 * s, d)) for r in range(s, c, 2 * s)], axis=0), bc)
        yield
        e = _each(lambda b, rf: jnp.exp(jnp.where(odd, b - rf, rf - b)), bc, ref)
        yield
        def keep_blocks(x, want_odd):
            zero = jnp.zeros((s, d), F32)
            return jnp.concatenate(
                [x[r:r + s] if ((r // s) % 2 == 1) == want_odd else zero for r in range(0, c, s)], axis=0)

        qt = _each(lambda q, e_: keep_blocks(q * e_, True).astype(BF16), qs, e)
        yield
        kt = _each(lambda k, e_: keep_blocks(k * e_, False).astype(BF16), ks, e)
        yield
        lvl = _each(lambda q_t, k_t: lax.dot_general(q_t, _tile_rows_masked(k_t, m["bd_sq"]), _NT,
                                                     preferred_element_type=F32), qt, kt)
        yield
        att = _each(lambda a_, l: a_ + jnp.where(m["pair", s], l, 0.0), att, lvl)
        yield

    def shifted_prods(q, k, b):
        out = []
        for o in range(SUBLANES):
            ko = k if o == 0 else _roll_in_groups(k, o)
            bo = b if o == 0 else _roll_in_groups(b, o)
            out.append((q * ko * jnp.exp(jnp.where(m["valid", o], b - bo, -jnp.inf))).astype(BF16))
        return jnp.concatenate(out, axis=0)

    prods = _each(shifted_prods, qs, ks, bc)
    yield
    dsum = _each(lambda pr: dot(pr, m["bd_sq"]), prods)
    yield
    for o in range(SUBLANES):
        att = _each(lambda a_, ds: a_ + jnp.where(m["diag", o], ds[o * c:(o + 1) * c], 0.0), att, dsum)
        yield
    vb = _each(lambda v: v.astype(BF16), vs)
    yield
    o_intra = _each(lambda a_, v_b: dot(a_.astype(BF16), _tile_rows_masked(v_b, m["bd_sq"])), att, vb)
    yield
    qdec = _each(lambda q, b: (q * jnp.exp(b)).astype(BF16), qs, bc)
    yield
    kdec = _each(lambda k, b: (k * jnp.exp(b[c - 1:c, :] - b)).astype(BF16), ks, bc)
    yield
    decay = _each(lambda b: jnp.exp(b[c - 1:c, :]), bc)
    yield
    return list(zip(o_intra, qdec, kdec, vb, decay))


def _hgrn_step(prep, st_ref, m):
    o_intra, qdec, kdec, vb, decay = prep
    st = st_ref[...]
    o = o_intra + lax.dot_general(qdec, st.astype(BF16), _NT, preferred_element_type=F32)
    upd = lax.dot_general(vb, kdec, _TN, preferred_element_type=F32)
    st_ref[...] = st * decay + upd * m["bd_sq"].astype(F32)
    return o


def _mix_tile(layer, tile, x_ref, y_ref, proj, cat, aconv_s, sa_s, bconv_s, bh_s, sc_s, w, masks):
    p_ref, win_a_ref, win_ba_ref, win_c_ref, wgate_ref, wout_ref = w
    d = D_C
    x = x_ref[...]
    h_in = _rms(x, p_ref[MP_NORM:MP_NORM + 1, 0:D_MODEL]).astype(BF16)
    yield

    def in_proj(lo, width):
        if lo < OFF_BA:
            w_cols = win_a_ref[:, lo:lo + width]
        elif lo == OFF_BA:
            w_cols = win_ba_ref[...]
        else:
            w_cols = win_c_ref[:, lo - OFF_BX:lo - OFF_BX + width]
        return jnp.dot(h_in, w_cols, preferred_element_type=F32)

    for j in range(3 * D_A // FFN_BLK):
        lo = j * FFN_BLK
        pre = in_proj(lo, FFN_BLK)
        yield
        act = _silu(_causal_conv(pre, aconv_s[:, lo:lo + FFN_BLK], p_ref[MP_ACONV:MP_ACONV + A_CONV, lo:lo + FFN_BLK]))
        aconv_s[:, lo:lo + FFN_BLK] = pre[tile - SUBLANES:tile]
        if lo < 2 * D_A:
            for hh in range(FFN_BLK // DK_A):
                blk = act[:, hh * DK_A:(hh + 1) * DK_A]
                nrm = blk * lax.rsqrt(jnp.sum(blk * blk, axis=-1, keepdims=True) + 1e-6)
                if lo < D_A:
                    nrm = nrm * (DK_A ** -0.5)
                proj[:, lo + hh * DK_A:lo + (hh + 1) * DK_A] = nrm
        else:
            proj[:, lo:lo + FFN_BLK] = act
        yield
    ba = in_proj(OFF_BA, LANES)
    proj[:, OFF_G:OFF_G + LANES] = (-jnp.exp(p_ref[MP_ALOG:MP_ALOG + 1, 0:LANES])) * jax.nn.softplus(
        ba + p_ref[MP_DT:MP_DT + 1, 0:LANES])
    proj[:, OFF_BA:OFF_BA + LANES] = jax.nn.sigmoid(ba)
    yield

    bpre = in_proj(OFF_BX, D_B)
    yield
    xb = _causal_conv(bpre, bconv_s[...], p_ref[MP_BCONV:MP_BCONV + B_CONV, 0:D_B]) + p_ref[MP_BCONVB:MP_BCONVB + 1, 0:D_B]
    bconv_s[...] = bpre[tile - SUBLANES:tile]
    gates = _mm(xb, wgate_ref[...])
    r_gate = jax.nn.sigmoid(gates[:, :D_B] + p_ref[MP_BR:MP_BR + 1, 0:D_B])
    i_gate = jax.nn.sigmoid(gates[:, D_B:] + p_ref[MP_BI:MP_BI + 1, 0:D_B])
    yield
    log_a = (-RG_C) * r_gate * jax.nn.softplus(-p_ref[MP_LAMBDA:MP_LAMBDA + 1, 0:D_B])
    a_gate = jnp.exp(log_a)
    one_m_a2 = -jnp.tanh(log_a) * (a_gate * a_gate + 1.0)
    mult = jnp.where(one_m_a2 > 0.0, one_m_a2 * lax.rsqrt(one_m_a2), 0.0)
    yield
    for lo in (OFF_Z, OFF_Z + FFN_BLK, OFF_CI, OFF_CG):
        proj[:, lo:lo + FFN_BLK] = in_proj(lo, FFN_BLK)
        yield
    hb = _lin_scan(a_gate, mult * i_gate * xb, bh_s[...])
    bh_s[...] = hb[tile - 1:tile]
    yield
    cat[:, D_A:D_A + D_B] = hb * jax.nn.gelu(in_proj(OFF_BG, D_B))
    yield

    lg = p_ref[MP_CLB:MP_CLB + N_LAYERS, 0:d]
    ex = jnp.exp(lg - jnp.max(lg, axis=0, keepdims=True))
    sm = ex / jnp.sum(ex, axis=0, keepdims=True)
    lb = jnp.sum(sm[0:layer + 1], axis=0, keepdims=True) - sm[0:1]
    forget = lb + (1.0 - lb) * jax.nn.sigmoid(in_proj(OFF_CF, d))
    proj[:, OFF_CK:OFF_CK + d] = 1.0 - forget
    proj[:, OFF_CF:OFF_CF + d] = jnp.log(forget)
    yield
    proj[:, OFF_CQ:OFF_CQ + d] = _silu(in_proj(OFF_CQ, d))
    yield

    chunks = [slice(ci * CHUNK, (ci + 1) * CHUNK) for ci in range(tile // CHUNK)]

    def cols(lo, width):
        return [proj[rows, lo:lo + width] for rows in chunks]

    gdn = _gdn_prep(cols(0, D_A), cols(D_A, D_A), cols(2 * D_A, D_A), cols(OFF_BA, LANES), cols(OFF_G, LANES), masks)
    hgrn = _hgrn_prep(cols(OFF_CQ, d), cols(OFF_CK, d), cols(OFF_CI, d), cols(OFF_CF, d), masks)
    gdn_preps, hgrn_preps = yield from _interleave(gdn, hgrn)
    for rows, gdn_prep, hgrn_prep in zip(chunks, gdn_preps, hgrn_preps):
        cat[rows, 0:D_A] = _gdn_step(gdn_prep, sa_s, masks)
        yield
        cat[rows, D_A + D_B:D_MODEL] = _hgrn_step(hgrn_prep, sc_s, masks)
        yield

    for h in range(H_A):
        blk = cat[:, h * DV_A:(h + 1) * DV_A]
        ms = jnp.mean(blk * blk, axis=-1, keepdims=True)
        z = proj[:, OFF_Z + h * DV_A:OFF_Z + (h + 1) * DV_A]
        cat[:, h * DV_A:(h + 1) * DV_A] = (blk * lax.rsqrt(ms + RMS_EPS) * p_ref[MP_ANORM:MP_ANORM + 1, 0:DV_A]
                                           * _silu(z))
        yield
    oc = cat[:, D_A + D_B:D_MODEL]
    ms_c = _mm_sel(oc * oc, masks["bd_sq"], parts=2) * (1.0 / DK_C)
    cat[:, D_A + D_B:D_MODEL] = (oc * lax.rsqrt(ms_c + RMS_EPS) * p_ref[MP_CNORM:MP_CNORM + 1, 0:d]
                                 * _silu(proj[:, OFF_CG:OFF_CG + d]))
    yield
    y_ref[...] = x + _mm(cat[...], wout_ref[...])


def _mix_kernel(layer, zero_state, *refs):
    n_state_in = 0 if zero_state else 5
    x_ref = refs[0]
    state_in = refs[1:1 + n_state_in]
    w = refs[1 + n_state_in:7 + n_state_in]
    y_ref, aconv_o, as_o, bconv_o, bh_o, cs_o, proj, cat, aconv_s, sa_s, bconv_s, bh_s, sc_s = refs[7 + n_state_in:]
    t = pl.program_id(1)
    nt = pl.num_programs(1)
    n_inst, tile = x_ref.shape[0], x_ref.shape[1]
    d = D_C
    masks = _chunk_masks()

    @pl.when(t == 0)
    def _load_state():
        if zero_state:
            for ref in (aconv_s, sa_s, bconv_s, bh_s, sc_s):
                ref[...] = jnp.zeros(ref.shape, F32)
            return
        aconv_ref, as_ref, bconv_ref, bh_ref, cs_ref = state_in
        for i in range(n_inst):
            aconv_s[i] = aconv_ref[i]
            bconv_s[i] = bconv_ref[i]
            bh_s[i] = bh_ref[i]
            for h in range(H_A):
                sa_s[i, :, h * DV_A:(h + 1) * DV_A] = as_ref[i, h]
            rows = jnp.concatenate([cs_ref[i, h] for h in range(H_C)], axis=0)
            tile_sel = jnp.where(_iota((DK_C, d), 1) % DK_C == _iota((DK_C, d), 0), 1.0, 0.0).astype(BF16)
            s_bd = _mm_sel(rows, tile_sel) * masks["bd_sq"].astype(F32)
            sc_s[i] = s_bd.T

    _interleave_skewed(
        [_mix_tile(layer, tile, x_ref.at[i], y_ref.at[i], proj.at[i], cat.at[i], aconv_s.at[i], sa_s.at[i],
                   bconv_s.at[i], bh_s.at[i], sc_s.at[i], w, masks) for i in range(n_inst)], INST_SKEW)

    @pl.when(t == nt - 1)
    def _store_state():
        for i in range(n_inst):
            aconv_o[i] = aconv_s[i][SUBLANES - (A_CONV - 1):]
            bconv_o[i] = bconv_s[i][SUBLANES - (B_CONV - 1):]
            bh_o[i] = bh_s[i]
            for h in range(H_A):
                as_o[i, h] = sa_s[i, :, h * DV_A:(h + 1) * DV_A]
            s_bd = sc_s[i].T
            for h in range(H_C):
                sel = jnp.where(_iota((d, DK_C), 0) == _iota((d, DK_C), 1) + h * DK_C, 1.0, 0.0).astype(BF16)
                cs_o[i, h] = _mm_sel(s_bd[h * DK_C:(h + 1) * DK_C, :], sel)


def _ffn_kernel(final, zero_state, kv_packed, *refs):
    n_state_in = 0 if zero_state else 1
    x_ref, mk_ref, mv_ref = refs[:3]
    state_in = refs[3:3 + n_state_in]
    p_ref, wq_ref, wo_ref, wup_ref, wdown_ref, y_ref, fconv_o = refs[3 + n_state_in:10 + n_state_in]
    scratch = refs[10 + n_state_in:]
    kb_s, vb_s = (mk_ref, mv_ref) if kv_packed else scratch[:2]
    carry_s, act_s = scratch[-2:]
    t = pl.program_id(1)
    nt = pl.num_programs(1)
    n_rows, tile = x_ref.shape[0], x_ref.shape[1]

    @pl.when(t == 0)
    def _load():
        if not kv_packed:
            for r in range(n_rows):
                for h in range(MEM_HEADS):
                    kb_s[r, :, h * MEM_HEAD_DIM:(h + 1) * MEM_HEAD_DIM] = mk_ref[r, :, h, :].astype(BF16)
                    vb_s[r, :, h * MEM_HEAD_DIM:(h + 1) * MEM_HEAD_DIM] = mv_ref[r, :, h, :].astype(BF16)
        if zero_state:
            carry_s[...] = jnp.zeros(carry_s.shape, F32)
        else:
            carry_s[...] = state_in[0][...]

    x = x_ref[...].reshape(n_rows * tile, D_MODEL)
    hq = _rms(x, p_ref[FP_NORM_ATTN:FP_NORM_ATTN + 1, 0:D_MODEL]).astype(BF16)
    q = jnp.dot(hq, wq_ref[...], preferred_element_type=F32)
    units = [(r, slice(r * tile, (r + 1) * tile), slice(h * MEM_HEAD_DIM, (h + 1) * MEM_HEAD_DIM))
             for r in range(n_rows) for h in range(MEM_HEADS)]
    scores = [_mm_nt(q[rs, cs], kb_s[r, :, cs]) * (MEM_HEAD_DIM ** -0.5) for r, rs, cs in units]
    expd = [jnp.exp(s - jnp.max(s, axis=-1, keepdims=True)) for s in scores]
    probs = [e / jnp.sum(e, axis=-1, keepdims=True) for e in expd]
    outs = [_mm(p, vb_s[r, :, cs]) for p, (r, rs, cs) in zip(probs, units)]
    attn = jnp.concatenate([jnp.concatenate(outs[r * MEM_HEADS:(r + 1) * MEM_HEADS], axis=1) for r in range(n_rows)],
                           axis=0)
    x1 = x + _mm(attn, wo_ref[...])

    hf = _rms(x1, p_ref[FP_NORM_FFN:FP_NORM_FFN + 1, 0:D_MODEL]).astype(BF16)
    for j in range(N_FFN_BLK):
        halves = []
        for off in (j * FFN_BLK, D_FF + j * FFN_BLK):
            u = jnp.dot(hf, wup_ref[:, off:off + FFN_BLK], preferred_element_type=F32)
            halves.append(_causal_conv(u, [carry_s[r, :, off:off + FFN_BLK] for r in range(n_rows)],
                                       p_ref[FP_CONV:FP_CONV + FFN_CONV, off:off + FFN_BLK])
                          + p_ref[FP_CONVB:FP_CONVB + 1, off:off + FFN_BLK])
            for r in range(n_rows):
                carry_s[r, :, off:off + FFN_BLK] = u[(r + 1) * tile - SUBLANES:(r + 1) * tile]
        act_s[:, j * FFN_BLK:(j + 1) * FFN_BLK] = (_silu(halves[0]) * halves[1]).astype(BF16)
    y = x1 + jnp.dot(act_s[...], wdown_ref[...], preferred_element_type=F32)
    if final:
        y = _rms(y, p_ref[FP_NORM_FINAL:FP_NORM_FINAL + 1, 0:D_MODEL])
    y_ref[...] = y.reshape(n_rows, tile, D_MODEL)

    @pl.when(t == nt - 1)
    def _store():
        for r in range(n_rows):
            fconv_o[r] = carry_s[r, SUBLANES - (FFN_CONV - 1):, :]


def _memkv_kernel(mem_ref, g_ref, wk_ref, wv_ref, k_ref, v_ref, kb_ref, vb_ref):
    mn = _rms(mem_ref[...], g_ref[...]).astype(BF16)
    k = jnp.dot(mn, wk_ref[...], preferred_element_type=F32)
    v = jnp.dot(mn, wv_ref[...], preferred_element_type=F32)
    kb_ref[...] = k.astype(BF16)
    vb_ref[...] = v.astype(BF16)
    for h in range(MEM_HEADS):
        k_ref[:, h, :] = k[:, h * MEM_HEAD_DIM:(h + 1) * MEM_HEAD_DIM]
        v_ref[:, h, :] = v[:, h * MEM_HEAD_DIM:(h + 1) * MEM_HEAD_DIM]


def _layer_resident(layer, shape):
    nd = len(shape)
    return pl.BlockSpec((None,) + tuple(shape[1:]), lambda b, t: (layer,) + (0,) * (nd - 1),
                        pipeline_mode=pl.Buffered(1))


def _layer_rows(layer, shape, n_rows):
    nd = len(shape)
    return pl.BlockSpec((None, n_rows) + tuple(shape[2:]), lambda g, t: (layer, g) + (0,) * (nd - 2))


def _rows_out(shape, n_rows):
    nd = len(shape)
    return pl.BlockSpec((n_rows,) + tuple(shape[1:]), lambda g, t: (g,) + (0,) * (nd - 1))


def _seq_tile(length):
    return 256 if length % 256 == 0 else CHUNK


def _mix_call(layer, x, states, wts):
    bsz, length, _ = x.shape
    tile = _seq_tile(length)
    x_spec = pl.BlockSpec((N_INST, tile, D_MODEL), lambda g, t: (g, t, 0))
    state_shapes = [(bsz, A_CONV - 1, 3 * D_A), (bsz, H_A, DK_A, DV_A), (bsz, B_CONV - 1, D_B), (bsz, 1, D_B),
                    (bsz, H_C, DK_C, DK_C)]
    state_in = [] if states is None else list(states)
    return pl.pallas_call(
        functools.partial(_mix_kernel, layer, states is None),
        grid=(bsz // N_INST, length // tile),
        in_specs=[x_spec] + [_layer_rows(layer, s.shape, N_INST) for s in state_in]
        + [_layer_resident(layer, w.shape) for w in wts],
        out_specs=[x_spec] + [_rows_out(shp, N_INST) for shp in state_shapes],
        out_shape=[jax.ShapeDtypeStruct(x.shape, F32)] + [jax.ShapeDtypeStruct(shp, F32) for shp in state_shapes],
        scratch_shapes=[
            pltpu.VMEM((N_INST, tile, PROJ_W), F32),
            pltpu.VMEM((N_INST, tile, D_MODEL), F32),
            pltpu.VMEM((N_INST, SUBLANES, 3 * D_A), F32),
            pltpu.VMEM((N_INST, DK_A, H_A * DV_A), F32),
            pltpu.VMEM((N_INST, SUBLANES, D_B), F32),
            pltpu.VMEM((N_INST, 1, D_B), F32),
            pltpu.VMEM((N_INST, D_C, D_C), F32),
        ],
        compiler_params=pltpu.CompilerParams(
            dimension_semantics=("arbitrary", "arbitrary"), vmem_limit_bytes=VMEM_LIMIT_BYTES),
        name="mix_block",
    )(x, *state_in, *wts)


def _ffn_call(layer, final, x, mem_k, mem_v, fconv, wts):
    bsz, length, _ = x.shape
    tile = FFN_TILE if length % FFN_TILE == 0 else _seq_tile(length)
    n_rows = math.gcd(bsz, max(1, FFN_ROWS // tile))
    x_spec = pl.BlockSpec((n_rows, tile, D_MODEL), lambda g, t: (g, t, 0))
    kv_packed = mem_k.dtype == BF16
    kv_buffers = {} if n_rows == 1 else {"pipeline_mode": pl.Buffered(1)}
    kv_spec = pl.BlockSpec((None, n_rows) + tuple(mem_k.shape[2:]),
                           lambda g, t: (layer, g) + (0,) * (mem_k.ndim - 2), **kv_buffers)
    kv_scratch = [] if kv_packed else [pltpu.VMEM((n_rows, N_MEM, D_MODEL), BF16)] * 2
    state_in = [] if fconv is None else [fconv]
    fconv_shape = (bsz, FFN_CONV - 1, 2 * D_FF)
    return pl.pallas_call(
        functools.partial(_ffn_kernel, final, fconv is None, kv_packed),
        grid=(bsz // n_rows, length // tile),
        in_specs=[x_spec, kv_spec, kv_spec] + [_layer_rows(layer, s.shape, n_rows) for s in state_in]
        + [_layer_resident(layer, w.shape) for w in wts],
        out_specs=[x_spec, _rows_out(fconv_shape, n_rows)],
        out_shape=[jax.ShapeDtypeStruct(x.shape, F32), jax.ShapeDtypeStruct(fconv_shape, F32)],
        scratch_shapes=kv_scratch + [
            pltpu.VMEM((n_rows, SUBLANES, 2 * D_FF), F32),
            pltpu.VMEM((n_rows * tile, D_FF), BF16),
        ],
        compiler_params=pltpu.CompilerParams(
            dimension_semantics=("arbitrary", "arbitrary"), vmem_limit_bytes=VMEM_LIMIT_BYTES),
        name="attn_ffn_block",
    )(x, mem_k, mem_v, *state_in, *wts)


def _memkv_call(mem, norm_mem, wk, wv):
    depth = wk.shape[0]
    bsz, n_mem, _ = mem.shape
    w_spec = pl.BlockSpec((None, D_MODEL, D_MODEL), lambda l, b: (l, 0, 0))
    o_spec = pl.BlockSpec((None, None, n_mem, MEM_HEADS, MEM_HEAD_DIM), lambda l, b: (l, b, 0, 0, 0))
    shape = jax.ShapeDtypeStruct((depth, bsz, n_mem, MEM_HEADS, MEM_HEAD_DIM), F32)
    b_spec = pl.BlockSpec((None, None, n_mem, D_MODEL), lambda l, b: (l, b, 0, 0))
    b_shape = jax.ShapeDtypeStruct((depth, bsz, n_mem, D_MODEL), BF16)
    return pl.pallas_call(
        _memkv_kernel,
        grid=(depth, bsz),
        in_specs=[pl.BlockSpec((None, n_mem, D_MODEL), lambda l, b: (b, 0, 0)),
                  pl.BlockSpec((None, 1, D_MODEL), lambda l, b: (l, 0, 0)), w_spec, w_spec],
        out_specs=[o_spec, o_spec, b_spec, b_spec],
        out_shape=[shape, shape, b_shape, b_shape],
        compiler_params=pltpu.CompilerParams(
            dimension_semantics=("arbitrary", "arbitrary"), vmem_limit_bytes=VMEM_LIMIT_BYTES),
        name="memory_kv",
    )(mem, norm_mem.reshape(depth, 1, D_MODEL), wk, wv)


def _pack_rows(pieces, width):
    padded = [jnp.pad(p, ((0, 0), (0, 0), (0, width - p.shape[2]))) for p in pieces]
    rows = sum(p.shape[1] for p in pieces)
    fill = jnp.zeros((pieces[0].shape[0], -rows % SUBLANES, width), F32)
    return jnp.concatenate(padded + [fill], axis=1)


def kernel(x_prompt, x_sample, cache_mem_k, cache_mem_v, state_a_conv, state_a_S, state_b_conv, state_b_h, state_c_S, state_ffn_conv, mem_prompt, norm_mix, w_in, a_conv_w, a_A_log, a_dt_bias, a_norm, b_conv_w, b_conv_b, b_w_r, b_b_r, b_w_i, b_b_i, b_lambda, c_lb_logits, c_norm, w_out, norm_attn, norm_mem, w_q, w_k, w_v, w_o, norm_ffn, w_up, ffn_conv_w, ffn_conv_b, w_down, norm_final):
    depth = w_in.shape[0]
    assert depth == N_LAYERS and x_prompt.shape[0] % N_INST == 0 and x_sample.shape[0] % N_INST == 0

    def rows(v):
        return v[:, None, :]

    w_in_a = w_in[:, :, :OFF_BA].astype(BF16)
    w_in_ba = jnp.pad(w_in[:, :, OFF_BA:OFF_BA + 2 * H_A], ((0, 0), (0, 0), (0, LANES - 2 * H_A))).astype(BF16)
    w_in_c = w_in[:, :, OFF_BA + 2 * H_A:].astype(BF16)
    eye = jnp.eye(B_BLOCKS, dtype=F32)
    w_gate = jnp.concatenate(
        [jnp.einsum('lnij,nm->lnimj', wg, eye).reshape(depth, D_B, D_B) for wg in (b_w_r, b_w_i)], axis=2).astype(BF16)
    lane_pad = jnp.zeros((depth, 1, H_A), F32)
    mix_p = _pack_rows([
        a_conv_w, rows(norm_mix), jnp.concatenate([lane_pad, rows(a_A_log)], axis=2),
        jnp.concatenate([lane_pad, rows(a_dt_bias)], axis=2), rows(a_norm),
        b_conv_w, rows(b_conv_b), rows(b_b_r), rows(b_b_i), rows(b_lambda), rows(jnp.tile(c_norm, (1, H_C))),
        jnp.broadcast_to(c_lb_logits[None], (depth,) + c_lb_logits.shape)], 3 * D_A)
    ffn_p = _pack_rows([
        ffn_conv_w, rows(ffn_conv_b), rows(norm_attn), rows(norm_ffn),
        jnp.broadcast_to(norm_final[None, None, :], (depth, 1, D_MODEL))], 2 * D_FF)
    mix_w = [mix_p, w_in_a, w_in_ba, w_in_c, w_gate, w_out.astype(BF16)]
    ffn_w = [ffn_p, w_q.astype(BF16), w_o.astype(BF16), w_up.astype(BF16), w_down.astype(BF16)]

    p_mem_k, p_mem_v, p_mem_kb, p_mem_vb = _memkv_call(mem_prompt, norm_mem, w_k.astype(BF16), w_v.astype(BF16))

    def pad8(s):
        return jnp.pad(s, ((0, 0), (0, 0), (SUBLANES - s.shape[2], 0), (0, 0)))

    def run_group(x, mem_k, mem_v, states):
        if states is not None:
            a_conv, a_s, b_conv, b_h, c_s, f_conv = states
            mix_states = [pad8(a_conv), a_s, pad8(b_conv), b_h[:, :, None, :], c_s]
            f_conv = pad8(f_conv)
        else:
            mix_states = f_conv = None
        outs = []
        for l in range(depth):
            x, na_conv, na_s, nb_conv, nb_h, nc_s = _mix_call(l, x, mix_states, mix_w)
            x, nf_conv = _ffn_call(l, l == depth - 1, x, mem_k, mem_v, f_conv, ffn_w)
            outs.append((na_conv, na_s, nb_conv, nb_h[:, 0], nc_s, nf_conv))
        return x, [jnp.stack(t, axis=0) for t in zip(*outs)]

    y_prompt, p_states = run_group(x_prompt, p_mem_kb, p_mem_vb, None)
    y_sample, s_states = run_group(
        x_sample, cache_mem_k, cache_mem_v,
        (state_a_conv, state_a_S, state_b_conv, state_b_h, state_c_S, state_ffn_conv))
    return (y_prompt, y_sample, p_mem_k, p_mem_v, *p_states, *s_states)
```

```python
import functools
import math

import jax
import jax.numpy as jnp
from jax import lax
from jax.experimental import pallas as pl
from jax.experimental.pallas import tpu as pltpu

F32 = jnp.float32
BF16 = jnp.bfloat16

D_MODEL = 1024
CHUNK = 64
D_A = 512
DK_A = 128
DV_A = 128
H_A = 4
A_CONV = 4
D_B = 256
B_BLOCKS = 4
B_CONV = 4
RG_C = 8.0
D_C = 256
H_C = 4
DK_C = 64
N_MEM = 256
MEM_HEADS = 4
MEM_HEAD_DIM = 256
D_FF = 2816
FFN_CONV = 3
RMS_EPS = 1e-6

LANES = 128
SUBLANES = 8
VMEM_LIMIT_BYTES = 56 * 1024 * 1024

OFF_QKV = 0
OFF_Z = 1536
OFF_BA = 2048
OFF_BX = 2176
OFF_BG = 2432
OFF_CQ = 2688
OFF_CF = 2944
OFF_CI = 3200
OFF_CG = 3456
N_PROJ = 3712
OFF_CK = 3712
OFF_G = 3968
PROJ_W = 4096
FFN_BLK = 256
N_FFN_BLK = D_FF // FFN_BLK
FFN_TILE = 1024
FFN_ROWS = 256

N_LAYERS = 2

MP_ACONV, MP_NORM, MP_ALOG, MP_DT, MP_ANORM = 0, 4, 5, 6, 7
MP_BCONV, MP_BCONVB, MP_BR, MP_BI, MP_LAMBDA, MP_CNORM, MP_CLB = 8, 12, 13, 14, 15, 16, 17
FP_CONV, FP_CONVB, FP_NORM_ATTN, FP_NORM_FFN, FP_NORM_FINAL = 0, 3, 4, 5, 6

N_INST = 4
INST_SKEW = 12

_NT = (((1,), (1,)), ((), ()))
_TN = (((0,), (0,)), ((), ()))


def _mm(a, b):
    return jnp.dot(a.astype(BF16), b.astype(BF16), preferred_element_type=F32)


def _mm_nt(a, b):
    return lax.dot_general(a.astype(BF16), b.astype(BF16), _NT, preferred_element_type=F32)


def _split3(x):
    hi = x.astype(BF16)
    r = x - hi.astype(F32)
    mid = r.astype(BF16)
    lo = (r - mid.astype(F32)).astype(BF16)
    return hi, mid, lo


def _mm_sel(x, sel, parts=3):
    return sum(jnp.dot(part, sel, preferred_element_type=F32) for part in _split3(x)[:parts])


def _cumsum_rows(x):
    c = x.shape[0]
    tri3 = ((_iota((c, 3 * c), 1) % c) <= _iota((c, 3 * c), 0)).astype(BF16)
    return jnp.dot(tri3, jnp.concatenate(_split3(x), axis=0), preferred_element_type=F32)


def _rms(x, g):
    ms = jnp.mean(x * x, axis=-1, keepdims=True)
    return x * lax.rsqrt(ms + RMS_EPS) * g


def _silu(x):
    return x * jax.nn.sigmoid(x)


def _iota(shape, axis):
    return lax.broadcasted_iota(jnp.int32, shape, axis)


def _shift_rows(x, carries, s):
    seg = x.shape[0] // len(carries)
    rolled = pltpu.roll(x, s, 0)
    pieces = []
    for r, carry8 in enumerate(carries):
        lo = r * seg
        pieces.append(jnp.where(_iota(carry8.shape, 0) < s, pltpu.roll(carry8, s, 0), rolled[lo:lo + SUBLANES]))
        if seg > SUBLANES:
            pieces.append(rolled[lo + SUBLANES:lo + seg])
    return pieces[0] if len(pieces) == 1 else jnp.concatenate(pieces, axis=0)


def _causal_conv(x, carries, w):
    if not isinstance(carries, (list, tuple)):
        carries = [carries]
    width = w.shape[0]
    y = x * w[width - 1:width, :]
    for s in range(1, width):
        y = y + _shift_rows(x, carries, s) * w[width - 1 - s:width - s, :]
    return y


def _roll_in_groups(x, s):
    n, width = x.shape
    return pltpu.roll(x.reshape(n // SUBLANES, SUBLANES, width), s, 1).reshape(n, width)


def _lin_scan(a, u, h0):
    n = a.shape[0]
    row8 = _iota(a.shape, 0) % SUBLANES
    s = 1
    while s < SUBLANES:
        keep = row8 >= s
        a_s = jnp.where(keep, _roll_in_groups(a, s), 1.0)
        u_s = jnp.where(keep, _roll_in_groups(u, s), 0.0)
        u = a * u_s + u
        a = a * a_s
        s *= 2
    groups = []
    carry = h0
    for r in range(0, n, SUBLANES):
        groups.append(a[r:r + SUBLANES] * carry + u[r:r + SUBLANES])
        carry = groups[-1][SUBLANES - 1:SUBLANES]
    return jnp.concatenate(groups, axis=0)


def _tile_rows_masked(x, mask):
    return jnp.concatenate([x] * (mask.shape[0] // x.shape[0]), axis=0) * mask


def _chunk_masks():
    c = CHUNK
    n = H_A * c
    rowi = _iota((c, n), 0)
    lane_j = _iota((c, n), 1) % c
    m = {
        "half": _iota((c, LANES), 1) < c,
        "incl": lane_j <= rowi,
        "strict": lane_j < rowi,
        "eye": jnp.where(lane_j == rowi, 1.0, 0.0),
        "bd_sq": jnp.where(_iota((n, n), 0) // c == _iota((n, n), 1) // c, 1.0, 0.0).astype(BF16),
        "bd_wide": jnp.where(_iota((n, H_A * DV_A), 0) // c == _iota((n, H_A * DV_A), 1) // DV_A,
                             1.0, 0.0).astype(BF16),
        "ones3": jnp.ones((c, 3 * c), BF16),
    }
    for s in (32, 16, 8):
        m["odd", s] = ((_iota((c, D_C), 0) // s) % 2) == 1
        m["pair", s] = (rowi // (2 * s)) == (lane_j // (2 * s))
    for o in range(SUBLANES):
        m["valid", o] = (_iota((c, D_C), 0) % SUBLANES) >= o
        m["diag", o] = lane_j == rowi - o
    return m


def _each(fn, *lists):
    return [fn(*args) for args in zip(*lists)]


def _interleave(*gens):
    results = [None] * len(gens)
    live = dict(enumerate(gens))
    while live:
        for i, g in list(live.items()):
            try:
                next(g)
            except StopIteration as done:
                results[i] = done.value
                del live[i]
        yield
    return results


def _interleave_skewed(gens, skew):
    live = list(gens)
    step = 0
    while live:
        for k, g in enumerate(gens):
            if g in live and step >= k * skew:
                try:
                    next(g)
                except StopIteration:
                    live.remove(g)
        step += 1


def _gdn_prep(qs, ks, vs, beta_blks, g_blks, m):
    c = CHUNK
    dot = functools.partial(jnp.dot, preferred_element_type=F32)
    gcum = _each(_cumsum_rows, g_blks)
    yield

    def lane_forms(blk, first):
        cols = [jnp.broadcast_to(blk[:, first + h:first + h + 1], (c, LANES)) for h in range(H_A)]
        wide = jnp.concatenate(cols, axis=1)
        sq = jnp.concatenate([jnp.where(m["half"], cols[0], cols[1]), jnp.where(m["half"], cols[2], cols[3])],
                             axis=1)
        return wide, sq

    g_forms = _each(lambda g: lane_forms(g, H_A), gcum)
    yield
    b_forms = _each(lambda b: lane_forms(b, 0), beta_blks)
    yield
    gr = _each(lambda gf: dot(m["ones3"], jnp.concatenate(_split3(gf[1] * m["eye"]), axis=0)), g_forms)
    yield
    gamma = _each(lambda gf, r: jnp.exp(jnp.where(m["incl"], gf[1] - r, -jnp.inf)), g_forms, gr)
    yield
    kb = _each(lambda k: k.astype(BF16), ks)
    yield
    qk_kk = _each(lambda q, k_b: lax.dot_general(jnp.concatenate([q.astype(BF16), k_b], axis=0),
                                                 _tile_rows_masked(k_b, m["bd_wide"]), _NT,
                                                 preferred_element_type=F32), qs, kb)
    yield
    qk = _each(lambda r, gm: (r[:c] * gm).astype(BF16), qk_kk, gamma)
    yield
    a = _each(lambda bf, r, gm: jnp.where(m["strict"], bf[1] * r[c:] * gm, 0.0), b_forms, qk_kk, gamma)
    yield
    x = _each(lambda a_: m["eye"] - a_, a)
    yield
    ab = _each(lambda a_: a_.astype(BF16), a)
    yield
    p = _each(lambda a_b: dot(a_b, _tile_rows_masked(a_b, m["bd_sq"])), ab)
    yield
    for it in range(5):
        pb = _each(lambda p_: p_.astype(BF16), p)
        yield
        p_bd = _each(lambda p_b: _tile_rows_masked(p_b, m["bd_sq"]), pb)
        yield
        if it < 4:
            both = _each(lambda x_, p_b, bd: dot(jnp.concatenate([x_.astype(BF16), p_b], axis=0), bd), x, pb, p_bd)
            x = _each(lambda x_, r: x_ + r[:c], x, both)
            p = _each(lambda r: r[c:], both)
        else:
            x = _each(lambda x_, bd: x_ + dot(x_.astype(BF16), bd), x, p_bd)
    xb = _each(lambda x_: x_.astype(BF16), x)
    yield
    eg = _each(lambda gf: jnp.exp(gf[0]), g_forms)
    yield
    u = _each(lambda x_b, v, bf: dot(x_b, _tile_rows_masked((v * bf[0]).astype(BF16), m["bd_wide"])),
              xb, vs, b_forms)
    yield
    w = _each(lambda x_b, k, bf, e: dot(x_b, _tile_rows_masked((k * (bf[0] * e)).astype(BF16), m["bd_wide"])),
              xb, ks, b_forms, eg)
    yield
    glast = _each(lambda g: jnp.concatenate(
        [jnp.broadcast_to(g[c - 1:c, H_A + h:H_A + h + 1], (1, DV_A)) for h in range(H_A)], axis=1), gcum)
    yield
    kd = _each(lambda k, gl, gf: (k * jnp.exp(gl - gf[0])).astype(BF16), ks, glast, g_forms)
    yield
    kd_stack = _each(lambda kd_: jnp.concatenate([kd_[:, h * DK_A:(h + 1) * DK_A] for h in range(H_A)], axis=0), kd)
    yield
    wq = _each(lambda w_, q, e: [
        jnp.concatenate([w_[:, h * DK_A:(h + 1) * DK_A], (q * e)[:, h * DK_A:(h + 1) * DK_A]], axis=0).astype(BF16)
        for h in range(H_A)], w, qs, eg)
    decay = _each(jnp.exp, glast)
    yield
    return list(zip(u, wq, qk, kd_stack, decay))


def _gdn_step(prep, s_ref, m):
    u, wq, qk, kd_stack, decay = prep
    c = CHUNK
    s = s_ref[...]
    sb = s.astype(BF16)
    v_new, o_state = [], []
    for h in range(H_A):
        hs = slice(h * DV_A, (h + 1) * DV_A)
        ws = jnp.dot(wq[h], sb[:, hs], preferred_element_type=F32)
        v_new.append(u[:, hs] - ws[:c])
        o_state.append(ws[c:])
    v_bd = _tile_rows_masked(jnp.concatenate(v_new, axis=1).astype(BF16), m["bd_wide"])
    o = jnp.concatenate(o_state, axis=1) + jnp.dot(qk, v_bd, preferred_element_type=F32)
    s_ref[...] = s * decay + lax.dot_general(kd_stack, v_bd, _TN, preferred_element_type=F32)
    return o


def _hgrn_prep(qs, ks, vs, lfs, m):
    c = CHUNK
    d = D_C
    dot = functools.partial(jnp.dot, preferred_element_type=F32)
    bc = _each(_cumsum_rows, lfs)
    yield
    att = [jnp.zeros((c, H_C * c), F32) for _ in qs]
    for s in (32, 16, 8):
        odd = m["odd", s]
        ref = _each(lambda b: jnp.concatenate(
            [jnp.broadcast_to(b[r:r + 1, :], (2 * s, d)) for r in range(s, c, 2 * s)], axis=0), bc)
        yield
        e = _each(lambda b, rf: jnp.exp(jnp.where(odd, b - rf, rf - b)), bc, ref)
        yield
        def keep_blocks(x, want_odd):
            zero = jnp.zeros((s, d), F32)
            return jnp.concatenate(
                [x[r:r + s] if ((r // s) % 2 == 1) == want_odd else zero for r in range(0, c, s)], axis=0)

        qt = _each(lambda q, e_: keep_blocks(q * e_, True).astype(BF16), qs, e)
        yield
        kt = _each(lambda k, e_: keep_blocks(k * e_, False).astype(BF16), ks, e)
        yield
        lvl = _each(lambda q_t, k_t: lax.dot_general(q_t, _tile_rows_masked(k_t, m["bd_sq"]), _NT,
                                                     preferred_element_type=F32), qt, kt)
        yield
        att = _each(lambda a_, l: a_ + jnp.where(m["pair", s], l, 0.0), att, lvl)
        yield

    def shifted_prods(q, k, b):
        out = []
        for o in range(SUBLANES):
            ko = k if o == 0 else _roll_in_groups(k, o)
            bo = b if o == 0 else _roll_in_groups(b, o)
            out.append((q * ko * jnp.exp(jnp.where(m["valid", o], b - bo, -jnp.inf))).astype(BF16))
        return jnp.concatenate(out, axis=0)

    prods = _each(shifted_prods, qs, ks, bc)
    yield
    dsum = _each(lambda pr: dot(pr, m["bd_sq"]), prods)
    yield
    for o in range(SUBLANES):
        att = _each(lambda a_, ds: a_ + jnp.where(m["diag", o], ds[o * c:(o + 1) * c], 0.0), att, dsum)
        yield
    vb = _each(lambda v: v.astype(BF16), vs)
    yield
    o_intra = _each(lambda a_, v_b: dot(a_.astype(BF16), _tile_rows_masked(v_b, m["bd_sq"])), att, vb)
    yield
    qdec = _each(lambda q, b: (q * jnp.exp(b)).astype(BF16), qs, bc)
    yield
    kdec = _each(lambda k, b: (k * jnp.exp(b[c - 1:c, :] - b)).astype(BF16), ks, bc)
    yield
    decay = _each(lambda b: jnp.exp(b[c - 1:c, :]), bc)
    yield
    return list(zip(o_intra, qdec, kdec, vb, decay))


def _hgrn_step(prep, st_ref, m):
    o_intra, qdec, kdec, vb, decay = prep
    st = st_ref[...]
    o = o_intra + lax.dot_general(qdec, st.astype(BF16), _NT, preferred_element_type=F32)
    upd = lax.dot_general(vb, kdec, _TN, preferred_element_type=F32)
    st_ref[...] = st * decay + upd * m["bd_sq"].astype(F32)
    return o


def _mix_tile(layer, tile, x_ref, y_ref, proj, cat, aconv_s, sa_s, bconv_s, bh_s, sc_s, w, masks):
    p_ref, win_a_ref, win_ba_ref, win_c_ref, wgate_ref, wout_ref = w
    d = D_C
    x = x_ref[...]
    h_in = _rms(x, p_ref[MP_NORM:MP_NORM + 1, 0:D_MODEL]).astype(BF16)
    yield

    def in_proj(lo, width):
        if lo < OFF_BA:
            w_cols = win_a_ref[:, lo:lo + width]
        elif lo == OFF_BA:
            w_cols = win_ba_ref[...]
        else:
            w_cols = win_c_ref[:, lo - OFF_BX:lo - OFF_BX + width]
        return jnp.dot(h_in, w_cols, preferred_element_type=F32)

    for j in range(3 * D_A // FFN_BLK):
        lo = j * FFN_BLK
        pre = in_proj(lo, FFN_BLK)
        yield
        act = _silu(_causal_conv(pre, aconv_s[:, lo:lo + FFN_BLK], p_ref[MP_ACONV:MP_ACONV + A_CONV, lo:lo + FFN_BLK]))
        aconv_s[:, lo:lo + FFN_BLK] = pre[tile - SUBLANES:tile]
        if lo < 2 * D_A:
            for hh in range(FFN_BLK // DK_A):
                blk = act[:, hh * DK_A:(hh + 1) * DK_A]
                nrm = blk * lax.rsqrt(jnp.sum(blk * blk, axis=-1, keepdims=True) + 1e-6)
                if lo < D_A:
                    nrm = nrm * (DK_A ** -0.5)
                proj[:, lo + hh * DK_A:lo + (hh + 1) * DK_A] = nrm
        else:
            proj[:, lo:lo + FFN_BLK] = act
        yield
    ba = in_proj(OFF_BA, LANES)
    proj[:, OFF_G:OFF_G + LANES] = (-jnp.exp(p_ref[MP_ALOG:MP_ALOG + 1, 0:LANES])) * jax.nn.softplus(
        ba + p_ref[MP_DT:MP_DT + 1, 0:LANES])
    proj[:, OFF_BA:OFF_BA + LANES] = jax.nn.sigmoid(ba)
    yield

    bpre = in_proj(OFF_BX, D_B)
    yield
    xb = _causal_conv(bpre, bconv_s[...], p_ref[MP_BCONV:MP_BCONV + B_CONV, 0:D_B]) + p_ref[MP_BCONVB:MP_BCONVB + 1, 0:D_B]
    bconv_s[...] = bpre[tile - SUBLANES:tile]
    gates = _mm(xb, wgate_ref[...])
    r_gate = jax.nn.sigmoid(gates[:, :D_B] + p_ref[MP_BR:MP_BR + 1, 0:D_B])
    i_gate = jax.nn.sigmoid(gates[:, D_B:] + p_ref[MP_BI:MP_BI + 1, 0:D_B])
    yield
    log_a = (-RG_C) * r_gate * jax.nn.softplus(-p_ref[MP_LAMBDA:MP_LAMBDA + 1, 0:D_B])
    a_gate = jnp.exp(log_a)
    one_m_a2 = -jnp.tanh(log_a) * (a_gate * a_gate + 1.0)
    mult = jnp.where(one_m_a2 > 0.0, one_m_a2 * lax.rsqrt(one_m_a2), 0.0)
    yield
    for lo in (OFF_Z, OFF_Z + FFN_BLK, OFF_CI, OFF_CG):
        proj[:, lo:lo + FFN_BLK] = in_proj(lo, FFN_BLK)
        yield
    hb = _lin_scan(a_gate, mult * i_gate * xb, bh_s[...])
    bh_s[...] = hb[tile - 1:tile]
    yield
    cat[:, D_A:D_A + D_B] = hb * jax.nn.gelu(in_proj(OFF_BG, D_B))
    yield

    lg = p_ref[MP_CLB:MP_CLB + N_LAYERS, 0:d]
    ex = jnp.exp(lg - jnp.max(lg, axis=0, keepdims=True))
    sm = ex / jnp.sum(ex, axis=0, keepdims=True)
    lb = jnp.sum(sm[0:layer + 1], axis=0, keepdims=True) - sm[0:1]
    forget = lb + (1.0 - lb) * jax.nn.sigmoid(in_proj(OFF_CF, d))
    proj[:, OFF_CK:OFF_CK + d] = 1.0 - forget
    proj[:, OFF_CF:OFF_CF + d] = jnp.log(forget)
    yield
    proj[:, OFF_CQ:OFF_CQ + d] = _silu(in_proj(OFF_CQ, d))
    yield

    chunks = [slice(ci * CHUNK, (ci + 1) * CHUNK) for ci in range(tile // CHUNK)]

    def cols(lo, width):
        return [proj[rows, lo:lo + width] for rows in chunks]

    gdn = _gdn_prep(cols(0, D_A), cols(D_A, D_A), cols(2 * D_A, D_A), cols(OFF_BA, LANES), cols(OFF_G, LANES), masks)
    hgrn = _hgrn_prep(cols(OFF_CQ, d), cols(OFF_CK, d), cols(OFF_CI, d), cols(OFF_CF, d), masks)
    gdn_preps, hgrn_preps = yield from _interleave(gdn, hgrn)
    for rows, gdn_prep, hgrn_prep in zip(chunks, gdn_preps, hgrn_preps):
        cat[rows, 0:D_A] = _gdn_step(gdn_prep, sa_s, masks)
        yield
        cat[rows, D_A + D_B:D_MODEL] = _hgrn_step(hgrn_prep, sc_s, masks)
        yield

    for h in range(H_A):
        blk = cat[:, h * DV_A:(h + 1) * DV_A]
        ms = jnp.mean(blk * blk, axis=-1, keepdims=True)
        z = proj[:, OFF_Z + h * DV_A:OFF_Z + (h + 1) * DV_A]
        cat[:, h * DV_A:(h + 1) * DV_A] = (blk * lax.rsqrt(ms + RMS_EPS) * p_ref[MP_ANORM:MP_ANORM + 1, 0:DV_A]
                                           * _silu(z))
        yield
    oc = cat[:, D_A + D_B:D_MODEL]
    ms_c = _mm_sel(oc * oc, masks["bd_sq"], parts=2) * (1.0 / DK_C)
    cat[:, D_A + D_B:D_MODEL] = (oc * lax.rsqrt(ms_c + RMS_EPS) * p_ref[MP_CNORM:MP_CNORM + 1, 0:d]
                                 * _silu(proj[:, OFF_CG:OFF_CG + d]))
    yield
    y_ref[...] = x + _mm(cat[...], wout_ref[...])


def _mix_kernel(layer, zero_state, *refs):
    n_state_in = 0 if zero_state else 5
    x_ref = refs[0]
    state_in = refs[1:1 + n_state_in]
    w = refs[1 + n_state_in:7 + n_state_in]
    y_ref, aconv_o, as_o, bconv_o, bh_o, cs_o, proj, cat, aconv_s, sa_s, bconv_s, bh_s, sc_s = refs[7 + n_state_in:]
    t = pl.program_id(1)
    nt = pl.num_programs(1)
    n_inst, tile = x_ref.shape[0], x_ref.shape[1]
    d = D_C
    masks = _chunk_masks()

    @pl.when(t == 0)
    def _load_state():
        if zero_state:
            for ref in (aconv_s, sa_s, bconv_s, bh_s, sc_s):
                ref[...] = jnp.zeros(ref.shape, F32)
            return
        aconv_ref, as_ref, bconv_ref, bh_ref, cs_ref = state_in
        for i in range(n_inst):
            aconv_s[i] = aconv_ref[i]
            bconv_s[i] = bconv_ref[i]
            bh_s[i] = bh_ref[i]
            for h in range(H_A):
                sa_s[i, :, h * DV_A:(h + 1) * DV_A] = as_ref[i, h]
            rows = jnp.concatenate([cs_ref[i, h] for h in range(H_C)], axis=0)
            tile_sel = jnp.where(_iota((DK_C, d), 1) % DK_C == _iota((DK_C, d), 0), 1.0, 0.0).astype(BF16)
            s_bd = _mm_sel(rows, tile_sel) * masks["bd_sq"].astype(F32)
            sc_s[i] = s_bd.T

    _interleave_skewed(
        [_mix_tile(layer, tile, x_ref.at[i], y_ref.at[i], proj.at[i], cat.at[i], aconv_s.at[i], sa_s.at[i],
                   bconv_s.at[i], bh_s.at[i], sc_s.at[i], w, masks) for i in range(n_inst)], INST_SKEW)

    @pl.when(t == nt - 1)
    def _store_state():
        for i in range(n_inst):
            aconv_o[i] = aconv_s[i][SUBLANES - (A_CONV - 1):]
            bconv_o[i] = bconv_s[i][SUBLANES - (B_CONV - 1):]
            bh_o[i] = bh_s[i]
            for h in range(H_A):
                as_o[i, h] = sa_s[i, :, h * DV_A:(h + 1) * DV_A]
            s_bd = sc_s[i].T
            for h in range(H_C):
                sel = jnp.where(_iota((d, DK_C), 0) == _iota((d, DK_C), 1) + h * DK_C, 1.0, 0.0).astype(BF16)
                cs_o[i, h] = _mm_sel(s_bd[h * DK_C:(h + 1) * DK_C, :], sel)


def _ffn_kernel(final, zero_state, kv_packed, *refs):
    n_state_in = 0 if zero_state else 1
    x_ref, mk_ref, mv_ref = refs[:3]
    state_in = refs[3:3 + n_state_in]
    p_ref, wq_ref, wo_ref, wup_ref, wdown_ref, y_ref, fconv_o = refs[3 + n_state_in:10 + n_state_in]
    scratch = refs[10 + n_state_in:]
    kb_s, vb_s = (mk_ref, mv_ref) if kv_packed else scratch[:2]
    carry_s, act_s = scratch[-2:]
    t = pl.program_id(1)
    nt = pl.num_programs(1)
    n_rows, tile = x_ref.shape[0], x_ref.shape[1]

    @pl.when(t == 0)
    def _load():
        if not kv_packed:
            for r in range(n_rows):
                for h in range(MEM_HEADS):
                    kb_s[r, :, h * MEM_HEAD_DIM:(h + 1) * MEM_HEAD_DIM] = mk_ref[r, :, h, :].astype(BF16)
                    vb_s[r, :, h * MEM_HEAD_DIM:(h + 1) * MEM_HEAD_DIM] = mv_ref[r, :, h, :].astype(BF16)
        if zero_state:
            carry_s[...] = jnp.zeros(carry_s.shape, F32)
        else:
            carry_s[...] = state_in[0][...]

    x = x_ref[...].reshape(n_rows * tile, D_MODEL)
    hq = _rms(x, p_ref[FP_NORM_ATTN:FP_NORM_ATTN + 1, 0:D_MODEL]).astype(BF16)
    q = jnp.dot(hq, wq_ref[...], preferred_element_type=F32)
    units = [(r, slice(r * tile, (r + 1) * tile), slice(h * MEM_HEAD_DIM, (h + 1) * MEM_HEAD_DIM))
             for r in range(n_rows) for h in range(MEM_HEADS)]
    scores = [_mm_nt(q[rs, cs], kb_s[r, :, cs]) * (MEM_HEAD_DIM ** -0.5) for r, rs, cs in units]
    expd = [jnp.exp(s - jnp.max(s, axis=-1, keepdims=True)) for s in scores]
    probs = [e / jnp.sum(e, axis=-1, keepdims=True) for e in expd]
    outs = [_mm(p, vb_s[r, :, cs]) for p, (r, rs, cs) in zip(probs, units)]
    attn = jnp.concatenate([jnp.concatenate(outs[r * MEM_HEADS:(r + 1) * MEM_HEADS], axis=1) for r in range(n_rows)],
                           axis=0)
    x1 = x + _mm(attn, wo_ref[...])

    hf = _rms(x1, p_ref[FP_NORM_FFN:FP_NORM_FFN + 1, 0:D_MODEL]).astype(BF16)
    for j in range(N_FFN_BLK):
        halves = []
        for off in (j * FFN_BLK, D_FF + j * FFN_BLK):
            u = jnp.dot(hf, wup_ref[:, off:off + FFN_BLK], preferred_element_type=F32)
            halves.append(_causal_conv(u, [carry_s[r, :, off:off + FFN_BLK] for r in range(n_rows)],
                                       p_ref[FP_CONV:FP_CONV + FFN_CONV, off:off + FFN_BLK])
                          + p_ref[FP_CONVB:FP_CONVB + 1, off:off + FFN_BLK])
            for r in range(n_rows):
                carry_s[r, :, off:off + FFN_BLK] = u[(r + 1) * tile - SUBLANES:(r + 1) * tile]
        act_s[:, j * FFN_BLK:(j + 1) * FFN_BLK] = (_silu(halves[0]) * halves[1]).astype(BF16)
    y = x1 + jnp.dot(act_s[...], wdown_ref[...], preferred_element_type=F32)
    if final:
        y = _rms(y, p_ref[FP_NORM_FINAL:FP_NORM_FINAL + 1, 0:D_MODEL])
    y_ref[...] = y.reshape(n_rows, tile, D_MODEL)

    @pl.when(t == nt - 1)
    def _store():
        for r in range(n_rows):
            fconv_o[r] = carry_s[r, SUBLANES - (FFN_CONV - 1):, :]


def _memkv_kernel(mem_ref, g_ref, wk_ref, wv_ref, k_ref, v_ref, kb_ref, vb_ref):
    mn = _rms(mem_ref[...], g_ref[...]).astype(BF16)
    k = jnp.dot(mn, wk_ref[...], preferred_element_type=F32)
    v = jnp.dot(mn, wv_ref[...], preferred_element_type=F32)
    kb_ref[...] = k.astype(BF16)
    vb_ref[...] = v.astype(BF16)
    for h in range(MEM_HEADS):
        k_ref[:, h, :] = k[:, h * MEM_HEAD_DIM:(h + 1) * MEM_HEAD_DIM]
        v_ref[:, h, :] = v[:, h * MEM_HEAD_DIM:(h + 1) * MEM_HEAD_DIM]


def _layer_resident(layer, shape):
    nd = len(shape)
    return pl.BlockSpec((None,) + tuple(shape[1:]), lambda b, t: (layer,) + (0,) * (nd - 1),
                        pipeline_mode=pl.Buffered(1))


def _layer_rows(layer, shape, n_rows):
    nd = len(shape)
    return pl.BlockSpec((None, n_rows) + tuple(shape[2:]), lambda g, t: (layer, g) + (0,) * (nd - 2))


def _rows_out(shape, n_rows):
    nd = len(shape)
    return pl.BlockSpec((n_rows,) + tuple(shape[1:]), lambda g, t: (g,) + (0,) * (nd - 1))


def _seq_tile(length):
    return 256 if length % 256 == 0 else CHUNK


def _mix_call(layer, x, states, wts):
    bsz, length, _ = x.shape
    tile = _seq_tile(length)
    x_spec = pl.BlockSpec((N_INST, tile, D_MODEL), lambda g, t: (g, t, 0))
    state_shapes = [(bsz, A_CONV - 1, 3 * D_A), (bsz, H_A, DK_A, DV_A), (bsz, B_CONV - 1, D_B), (bsz, 1, D_B),
                    (bsz, H_C, DK_C, DK_C)]
    state_in = [] if states is None else list(states)
    return pl.pallas_call(
        functools.partial(_mix_kernel, layer, states is None),
        grid=(bsz // N_INST, length // tile),
        in_specs=[x_spec] + [_layer_rows(layer, s.shape, N_INST) for s in state_in]
        + [_layer_resident(layer, w.shape) for w in wts],
        out_specs=[x_spec] + [_rows_out(shp, N_INST) for shp in state_shapes],
        out_shape=[jax.ShapeDtypeStruct(x.shape, F32)] + [jax.ShapeDtypeStruct(shp, F32) for shp in state_shapes],
        scratch_shapes=[
            pltpu.VMEM((N_INST, tile, PROJ_W), F32),
            pltpu.VMEM((N_INST, tile, D_MODEL), F32),
            pltpu.VMEM((N_INST, SUBLANES, 3 * D_A), F32),
            pltpu.VMEM((N_INST, DK_A, H_A * DV_A), F32),
            pltpu.VMEM((N_INST, SUBLANES, D_B), F32),
            pltpu.VMEM((N_INST, 1, D_B), F32),
            pltpu.VMEM((N_INST, D_C, D_C), F32),
        ],
        compiler_params=pltpu.CompilerParams(
            dimension_semantics=("arbitrary", "arbitrary"), vmem_limit_bytes=VMEM_LIMIT_BYTES),
        name="mix_block",
    )(x, *state_in, *wts)


def _ffn_call(layer, final, x, mem_k, mem_v, fconv, wts):
    bsz, length, _ = x.shape
    tile = FFN_TILE if length % FFN_TILE == 0 else _seq_tile(length)
    n_rows = math.gcd(bsz, max(1, FFN_ROWS // tile))
    x_spec = pl.BlockSpec((n_rows, tile, D_MODEL), lambda g, t: (g, t, 0))
    kv_packed = mem_k.dtype == BF16
    kv_buffers = {} if n_rows == 1 else {"pipeline_mode": pl.Buffered(1)}
    kv_spec = pl.BlockSpec((None, n_rows) + tuple(mem_k.shape[2:]),
                           lambda g, t: (layer, g) + (0,) * (mem_k.ndim - 2), **kv_buffers)
    kv_scratch = [] if kv_packed else [pltpu.VMEM((n_rows, N_MEM, D_MODEL), BF16)] * 2
    state_in = [] if fconv is None else [fconv]
    fconv_shape = (bsz, FFN_CONV - 1, 2 * D_FF)
    return pl.pallas_call(
        functools.partial(_ffn_kernel, final, fconv is None, kv_packed),
        grid=(bsz // n_rows, length // tile),
        in_specs=[x_spec, kv_spec, kv_spec] + [_layer_rows(layer, s.shape, n_rows) for s in state_in]
        + [_layer_resident(layer, w.shape) for w in wts],
        out_specs=[x_spec, _rows_out(fconv_shape, n_rows)],
        out_shape=[jax.ShapeDtypeStruct(x.shape, F32), jax.ShapeDtypeStruct(fconv_shape, F32)],
        scratch_shapes=kv_scratch + [
            pltpu.VMEM((n_rows, SUBLANES, 2 * D_FF), F32),
            pltpu.VMEM((n_rows * tile, D_FF), BF16),
        ],
        compiler_params=pltpu.CompilerParams(
            dimension_semantics=("arbitrary", "arbitrary"), vmem_limit_bytes=VMEM_LIMIT_BYTES),
        name="attn_ffn_block",
    )(x, mem_k, mem_v, *state_in, *wts)


def _memkv_call(mem, norm_mem, wk, wv):
    depth = wk.shape[0]
    bsz, n_mem, _ = mem.shape
    w_spec = pl.BlockSpec((None, D_MODEL, D_MODEL), lambda l, b: (l, 0, 0))
    o_spec = pl.BlockSpec((None, None, n_mem, MEM_HEADS, MEM_HEAD_DIM), lambda l, b: (l, b, 0, 0, 0))
    shape = jax.ShapeDtypeStruct((depth, bsz, n_mem, MEM_HEADS, MEM_HEAD_DIM), F32)
    b_spec = pl.BlockSpec((None, None, n_mem, D_MODEL), lambda l, b: (l, b, 0, 0))
    b_shape = jax.ShapeDtypeStruct((depth, bsz, n_mem, D_MODEL), BF16)
    return pl.pallas_call(
        _memkv_kernel,
        grid=(depth, bsz),
        in_specs=[pl.BlockSpec((None, n_mem, D_MODEL), lambda l, b: (b, 0, 0)),
                  pl.BlockSpec((None, 1, D_MODEL), lambda l, b: (l, 0, 0)), w_spec, w_spec],
        out_specs=[o_spec, o_spec, b_spec, b_spec],
        out_shape=[shape, shape, b_shape, b_shape],
        compiler_params=pltpu.CompilerParams(
            dimension_semantics=("arbitrary", "arbitrary"), vmem_limit_bytes=VMEM_LIMIT_BYTES),
        name="memory_kv",
    )(mem, norm_mem.reshape(depth, 1, D_MODEL), wk, wv)


def _pack_rows(pieces, width):
    padded = [jnp.pad(p, ((0, 0), (0, 0), (0, width - p.shape[2]))) for p in pieces]
    rows = sum(p.shape[1] for p in pieces)
    fill = jnp.zeros((pieces[0].shape[0], -rows % SUBLANES, width), F32)
    return jnp.concatenate(padded + [fill], axis=1)


def kernel(x_prompt, x_sample, cache_mem_k, cache_mem_v, state_a_conv, state_a_S, state_b_conv, state_b_h, state_c_S, state_ffn_conv, mem_prompt, norm_mix, w_in, a_conv_w, a_A_log, a_dt_bias, a_norm, b_conv_w, b_conv_b, b_w_r, b_b_r, b_w_i, b_b_i, b_lambda, c_lb_logits, c_norm, w_out, norm_attn, norm_mem, w_q, w_k, w_v, w_o, norm_ffn, w_up, ffn_conv_w, ffn_conv_b, w_down, norm_final):
    depth = w_in.shape[0]
    assert depth == N_LAYERS and x_prompt.shape[0] % N_INST == 0 and x_sample.shape[0] % N_INST == 0

    def rows(v):
        return v[:, None, :]

    w_in_a = w_in[:, :, :OFF_BA].astype(BF16)
    w_in_ba = jnp.pad(w_in[:, :, OFF_BA:OFF_BA + 2 * H_A], ((0, 0), (0, 0), (0, LANES - 2 * H_A))).astype(BF16)
    w_in_c = w_in[:, :, OFF_BA + 2 * H_A:].astype(BF16)
    eye = jnp.eye(B_BLOCKS, dtype=F32)
    w_gate = jnp.concatenate(
        [jnp.einsum('lnij,nm->lnimj', wg, eye).reshape(depth, D_B, D_B) for wg in (b_w_r, b_w_i)], axis=2).astype(BF16)
    lane_pad = jnp.zeros((depth, 1, H_A), F32)
    mix_p = _pack_rows([
        a_conv_w, rows(norm_mix), jnp.concatenate([lane_pad, rows(a_A_log)], axis=2),
        jnp.concatenate([lane_pad, rows(a_dt_bias)], axis=2), rows(a_norm),
        b_conv_w, rows(b_conv_b), rows(b_b_r), rows(b_b_i), rows(b_lambda), rows(jnp.tile(c_norm, (1, H_C))),
        jnp.broadcast_to(c_lb_logits[None], (depth,) + c_lb_logits.shape)], 3 * D_A)
    ffn_p = _pack_rows([
        ffn_conv_w, rows(ffn_conv_b), rows(norm_attn), rows(norm_ffn),
        jnp.broadcast_to(norm_final[None, None, :], (depth, 1, D_MODEL))], 2 * D_FF)
    mix_w = [mix_p, w_in_a, w_in_ba, w_in_c, w_gate, w_out.astype(BF16)]
    ffn_w = [ffn_p, w_q.astype(BF16), w_o.astype(BF16), w_up.astype(BF16), w_down.astype(BF16)]

    p_mem_k, p_mem_v, p_mem_kb, p_mem_vb = _memkv_call(mem_prompt, norm_mem, w_k.astype(BF16), w_v.astype(BF16))

    def pad8(s):
        return jnp.pad(s, ((0, 0), (0, 0), (SUBLANES - s.shape[2], 0), (0, 0)))

    def run_group(x, mem_k, mem_v, states):
        if states is not None:
            a_conv, a_s, b_conv, b_h, c_s, f_conv = states
            mix_states = [pad8(a_conv), a_s, pad8(b_conv), b_h[:, :, None, :], c_s]
            f_conv = pad8(f_conv)
        else:
            mix_states = f_conv = None
        outs = []
        for l in range(depth):
            x, na_conv, na_s, nb_conv, nb_h, nc_s = _mix_call(l, x, mix_states, mix_w)
            x, nf_conv = _ffn_call(l, l == depth - 1, x, mem_k, mem_v, f_conv, ffn_w)
            outs.append((na_conv, na_s, nb_conv, nb_h[:, 0], nc_s, nf_conv))
        return x, [jnp.stack(t, axis=0) for t in zip(*outs)]

    y_prompt, p_states = run_group(x_prompt, p_mem_kb, p_mem_vb, None)
    y_sample, s_states = run_group(
        x_sample, cache_mem_k, cache_mem_v,
        (state_a_conv, state_a_S, state_b_conv, state_b_h, state_c_S, state_ffn_conv))
    return (y_prompt, y_sample, p_mem_k, p_mem_v, *p_states, *s_states)
```
